```python
import math
import jax, jax.numpy as jnp
from jax import lax
import numpy as np

D_MODEL = 4096
BATCH = 4
SEQ = 2048
DEPTH = 2
DEC_BATCH = 8
DEC_SEQ = 1
PAST_LEN = 16384
PAGE_SIZE = 128

HEAD_DIM = 128
FOX_HEADS = 12
FOX_W = FOX_HEADS * HEAD_DIM
DIFF_HEADS = 6
DIFF_W = DIFF_HEADS * 2 * HEAD_DIM
POOL_WINDOWS = (2, 4, 8, 16)
POOL_GROUPS = len(POOL_WINDOWS)
POOL_W = D_MODEL // 4
POOL_GW = POOL_W // POOL_GROUPS
POOL_STATE = max(POOL_WINDOWS) - 1
D_FF = 11008
N_BRANCH = 3
N_MOD = 9
ROPE_THETA = 10000.0
Q_BLOCK = 128
EPS = 1e-6
FORGET_BIAS_INIT = 4.0

OFF_FQ = 0
OFF_FK = OFF_FQ + FOX_W
OFF_FV = OFF_FK + FOX_W
OFF_FF = OFF_FV + FOX_W
OFF_DQ = OFF_FF + FOX_HEADS
OFF_DK = OFF_DQ + DIFF_W
OFF_DV = OFF_DK + DIFF_W
OFF_U = OFF_DV + DIFF_W
N_IN = OFF_U + POOL_W

kernel_name = 'macaron_fox_diff_pool_hybrid_step'


def _rmsnorm(x, g):
    xf = x.astype(jnp.float32)
    y = xf * lax.rsqrt(jnp.mean(xf * xf, axis=-1, keepdims=True) + EPS)
    return (y * g.astype(jnp.float32)).astype(x.dtype)


def _rope(x, pos):
    half = HEAD_DIM // 2
    inv_freq = ROPE_THETA ** (-jnp.arange(half, dtype=jnp.float32) / half)
    ang = pos.astype(jnp.float32)[:, None] * inv_freq[None, :]
    cos = jnp.cos(ang)[None, :, None, :]
    sin = jnp.sin(ang)[None, :, None, :]
    xf = x.astype(jnp.float32)
    x1, x2 = xf[..., :half], xf[..., half:]
    return jnp.concatenate([x1 * cos - x2 * sin, x2 * cos + x1 * sin], axis=-1).astype(x.dtype)


def _sweep_query_blocks(fn, *q_arrays):
    sq = q_arrays[0].shape[1]
    qb = Q_BLOCK if sq % Q_BLOCK == 0 else sq
    nb = sq // qb
    blocks = tuple(jnp.moveaxis(a.reshape(a.shape[0], nb, qb, *a.shape[2:]), 1, 0) for a in q_arrays)
    out = lax.map(lambda blk: fn(*blk), blocks)
    out = jnp.moveaxis(out, 0, 1)
    return out.reshape(out.shape[0], sq, *out.shape[3:])


def _fox_attention(q, k, v, f_q, f_k, q_pos, k_pos):
    scale = HEAD_DIM ** -0.5
    fk_t = jnp.swapaxes(f_k, 1, 2)[:, :, None, :]

    def block(qb, fqb, pb):
        s = jnp.einsum('bqhd,bkhd->bhqk', qb, k).astype(jnp.float32) * scale
        s = s + jnp.swapaxes(fqb, 1, 2)[..., None] - fk_t
        mask = (k_pos[None, None, :] <= pb[:, :, None])[:, None]
        p = jax.nn.softmax(jnp.where(mask, s, -jnp.inf), axis=-1)
        return jnp.einsum('bhqk,bkhd->bqhd', p.astype(v.dtype), v)

    pos_b = jnp.broadcast_to(q_pos[None, :], q.shape[:2])
    return _sweep_query_blocks(block, q, f_q, pos_b)


def _diff_attention(q, k, v, q_pos, k_pos, lam):
    B, Sq = q.shape[:2]
    scale = HEAD_DIM ** -0.5
    q = q.reshape(B, Sq, DIFF_HEADS, 2, HEAD_DIM)
    k = k.reshape(k.shape[0], k.shape[1], DIFF_HEADS, 2, HEAD_DIM)

    def block(qb, pb):
        s = jnp.einsum('bqhcd,bkhcd->bchqk', qb, k).astype(jnp.float32) * scale
        mask = (k_pos[None, None, :] <= pb[:, :, None])[:, None, None]
        prob = jax.nn.softmax(jnp.where(mask, s, -jnp.inf), axis=-1)
        a = prob[:, 0] - lam * prob[:, 1]
        return jnp.einsum('bhqk,bkhe->bqhe', a.astype(v.dtype), v)

    pos_b = jnp.broadcast_to(q_pos[None, :], (B, Sq))
    return _sweep_query_blocks(block, q, pos_b)


def _pool_mix(u_ext, n_new):
    B, L, _ = u_ext.shape
    n_prev = L - n_new
    uf = u_ext.astype(jnp.float32)
    csum = jnp.concatenate([jnp.zeros((B, 1, POOL_W), jnp.float32), jnp.cumsum(uf, axis=1)], axis=1)
    j = jnp.arange(n_prev, L)
    hi = csum[:, j + 1]
    outs = []
    for g, w in enumerate(POOL_WINDOWS):
        lo_idx = jnp.maximum(j + 1 - w, 0)
        cnt = (j + 1 - lo_idx).astype(jnp.float32)
        sl = slice(g * POOL_GW, (g + 1) * POOL_GW)
        mean = (hi[..., sl] - csum[:, lo_idx][..., sl]) / cnt[None, :, None]
        outs.append(mean - uf[:, n_prev:, sl])
    return jnp.concatenate(outs, axis=-1).astype(u_ext.dtype)


def _swiglu(h, w_in, w_out):
    a = h @ w_in
    gate, up = jnp.split(a, 2, axis=-1)
    return (jax.nn.silu(gate) * up) @ w_out


def _token_mixer(h, p, layer, q_pos, past):
    B, S, _ = h.shape
    proj = h @ p['w_in']
    fq = proj[..., OFF_FQ:OFF_FK].reshape(B, S, FOX_HEADS, HEAD_DIM)
    fk = proj[..., OFF_FK:OFF_FV].reshape(B, S, FOX_HEADS, HEAD_DIM)
    fv = proj[..., OFF_FV:OFF_FF].reshape(B, S, FOX_HEADS, HEAD_DIM)
    logf = jax.nn.log_sigmoid(proj[..., OFF_FF:OFF_DQ].astype(jnp.float32)
                              + p['b_forget'].astype(jnp.float32))
    dq = _rope(proj[..., OFF_DQ:OFF_DK].reshape(B, S, 2 * DIFF_HEADS, HEAD_DIM), q_pos)
    dk = _rope(proj[..., OFF_DK:OFF_DV].reshape(B, S, 2 * DIFF_HEADS, HEAD_DIM), q_pos)
    dv = proj[..., OFF_DV:OFF_U].reshape(B, S, DIFF_HEADS, 2 * HEAD_DIM)
    u = proj[..., OFF_U:N_IN]
    cum_new = jnp.cumsum(logf, axis=1)

    if past is None:
        k_pos = q_pos
        all_fk, all_fv, f_key = fk, fv, cum_new
        all_dk, all_dv = dk, dv
        u_ext = u
    else:
        past_fk, past_fv, past_logf, past_dk, past_dv, past_u = past
        plogf = past_logf.astype(jnp.float32)
        suffix = lax.cumsum(plogf, axis=1, reverse=True) - plogf
        f_key = jnp.concatenate([-suffix, cum_new], axis=1)
        all_fk = jnp.concatenate([past_fk, fk], axis=1)
        all_fv = jnp.concatenate([past_fv, fv], axis=1)
        all_dk = jnp.concatenate([past_dk, dk], axis=1)
        all_dv = jnp.concatenate([past_dv, dv], axis=1)
        k_pos = jnp.arange(past_fk.shape[1] + S, dtype=jnp.int32)
        u_ext = jnp.concatenate([past_u, u], axis=1)

    y_fox = _fox_attention(fq, all_fk, all_fv, cum_new, f_key, q_pos, k_pos).reshape(B, S, FOX_W)

    lq = p['lambda_qk'].astype(jnp.float32)
    lam_init = 0.8 - 0.6 * math.exp(-0.3 * layer)
    lam = jnp.exp(jnp.sum(lq[0] * lq[1])) - jnp.exp(jnp.sum(lq[2] * lq[3])) + lam_init
    o = _diff_attention(dq, all_dk, all_dv, q_pos, k_pos, lam)
    y_diff = (_rmsnorm(o, p['subln_g']) * (1.0 - lam_init)).reshape(B, S, DIFF_W)

    pooled = _pool_mix(u_ext, S).reshape(B, S, POOL_GROUPS, POOL_GW)
    y_pool = jnp.einsum('bsgc,gcd->bsgd', pooled, p['w_pool']).reshape(B, S, POOL_W) * p['pool_scale']

    gates = jax.nn.sigmoid(h @ p['w_gate'] + p['b_gate']).reshape(B, S, N_BRANCH, D_MODEL)
    merged = (gates[:, :, 0] * (y_fox @ p['w_br_fox'])
              + gates[:, :, 1] * (y_diff @ p['w_br_diff'])
              + gates[:, :, 2] * (y_pool @ p['w_br_pool']))
    out = merged @ p['w_o']
    new_state = (fk, fv, logf, dk, dv, u_ext[:, -POOL_STATE:])
    return out, new_state


def _layer(x, c, p, layer, q_pos, past):
    B = x.shape[0]
    mod = (jax.nn.silu(c) @ p['w_ada'] + p['b_ada']).reshape(B, N_MOD, 1, D_MODEL)

    def pre(x_, i):
        return _rmsnorm(x_, p['norm_g'][i]) * (1 + mod[:, 3 * i + 1]) + mod[:, 3 * i]

    x = x + 0.5 * mod[:, 2] * _swiglu(pre(x, 0), p['w_ffn_in'][0], p['w_ffn_out'][0])
    m, st = _token_mixer(pre(x, 1), p, layer, q_pos, past)
    x = x + mod[:, 5] * m
    x = x + 0.5 * mod[:, 8] * _swiglu(pre(x, 2), p['w_ffn_in'][1], p['w_ffn_out'][1])
    return x, st


def setup_inputs(seed: int = 0) -> dict:
    key = jax.random.key(seed)
    ks = jax.random.split(key, 32)
    f32 = jnp.float32
    n_pages = PAST_LEN // PAGE_SIZE
    n_phys = (DEC_BATCH * n_pages * 5) // 4

    def nrm(k, shape, s):
        return jax.random.normal(k, shape, f32) * s

    page_table = jax.random.permutation(ks[5], n_phys)[: DEC_BATCH * n_pages].reshape(DEC_BATCH, n_pages).astype(jnp.int32)
    return {
        'x_prompt': nrm(ks[0], (BATCH, SEQ, D_MODEL), 1.0),
        'x_sample': nrm(ks[1], (DEC_BATCH, DEC_SEQ, D_MODEL), 1.0),
        'c_prompt': nrm(ks[2], (BATCH, D_MODEL), 1.0),
        'c_sample': nrm(ks[3], (DEC_BATCH, D_MODEL), 1.0),
        'page_table': page_table,
        'cache_fox_k': nrm(ks[6], (DEPTH, n_phys, PAGE_SIZE, FOX_HEADS, HEAD_DIM), 1.0),
        'cache_fox_v': nrm(ks[7], (DEPTH, n_phys, PAGE_SIZE, FOX_HEADS, HEAD_DIM), 1.0),
        'cache_fox_logf': jax.nn.log_sigmoid(FORGET_BIAS_INIT + nrm(ks[8], (DEPTH, n_phys, PAGE_SIZE, FOX_HEADS), 1.0)),
        'cache_diff_k': nrm(ks[9], (DEPTH, n_phys, PAGE_SIZE, 2 * DIFF_HEADS, HEAD_DIM), 1.0),
        'cache_diff_v': nrm(ks[10], (DEPTH, n_phys, PAGE_SIZE, DIFF_HEADS, 2 * HEAD_DIM), 1.0),
        'state_pool': nrm(ks[11], (DEPTH, DEC_BATCH, POOL_STATE, POOL_W), 1.0),
        'norm_g': 1.0 + nrm(ks[12], (DEPTH, 3, D_MODEL), 0.05),
        'w_ada': nrm(ks[13], (DEPTH, D_MODEL, N_MOD * D_MODEL), D_MODEL ** -0.5),
        'b_ada': nrm(ks[14], (DEPTH, N_MOD * D_MODEL), 0.02),
        'w_ffn_in': nrm(ks[15], (DEPTH, 2, D_MODEL, 2 * D_FF), D_MODEL ** -0.5),
        'w_ffn_out': nrm(ks[16], (DEPTH, 2, D_FF, D_MODEL), D_FF ** -0.5),
        'w_in': nrm(ks[17], (DEPTH, D_MODEL, N_IN), D_MODEL ** -0.5),
        'b_forget': FORGET_BIAS_INIT + nrm(ks[18], (DEPTH, FOX_HEADS), 0.1),
        'lambda_qk': nrm(ks[19], (DEPTH, 4, HEAD_DIM), 0.1),
        'subln_g': 1.0 + nrm(ks[20], (DEPTH, 2 * HEAD_DIM), 0.05),
        'w_pool': nrm(ks[21], (DEPTH, POOL_GROUPS, POOL_GW, POOL_GW), POOL_GW ** -0.5),
        'pool_scale': 1.0 + nrm(ks[22], (DEPTH, POOL_W), 0.1),
        'w_br_fox': nrm(ks[23], (DEPTH, FOX_W, D_MODEL), FOX_W ** -0.5),
        'w_br_diff': nrm(ks[24], (DEPTH, DIFF_W, D_MODEL), DIFF_W ** -0.5),
        'w_br_pool': nrm(ks[25], (DEPTH, POOL_W, D_MODEL), POOL_W ** -0.5),
        'w_gate': nrm(ks[26], (DEPTH, D_MODEL, N_BRANCH * D_MODEL), D_MODEL ** -0.5),
        'b_gate': nrm(ks[27], (DEPTH, N_BRANCH * D_MODEL), 0.02),
        'w_o': nrm(ks[28], (DEPTH, D_MODEL, D_MODEL), D_MODEL ** -0.5),
        'final_norm_g': 1.0 + nrm(ks[29], (D_MODEL,), 0.05),
    }


def reference(x_prompt, x_sample, c_prompt, c_sample, page_table, cache_fox_k, cache_fox_v,
              cache_fox_logf, cache_diff_k, cache_diff_v, state_pool, norm_g, w_ada, b_ada,
              w_ffn_in, w_ffn_out, w_in, b_forget, lambda_qk, subln_g, w_pool, pool_scale,
              w_br_fox, w_br_diff, w_br_pool, w_gate, b_gate, w_o, final_norm_g):
    n_pages = page_table.shape[1]
    past_len = n_pages * cache_fox_k.shape[2]
    pos_prompt = jnp.arange(x_prompt.shape[1], dtype=jnp.int32)
    pos_sample = past_len + jnp.arange(x_sample.shape[1], dtype=jnp.int32)

    def gather(cache_l):
        g = cache_l[page_table]
        return g.reshape(g.shape[0], g.shape[1] * g.shape[2], *g.shape[3:])

    xp, xs = x_prompt, x_sample
    st_p, st_s = [], []
    for l in range(DEPTH):
        p = dict(norm_g=norm_g[l], w_ada=w_ada[l], b_ada=b_ada[l], w_ffn_in=w_ffn_in[l],
                 w_ffn_out=w_ffn_out[l], w_in=w_in[l], b_forget=b_forget[l], lambda_qk=lambda_qk[l],
                 subln_g=subln_g[l], w_pool=w_pool[l], pool_scale=pool_scale[l],
                 w_br_fox=w_br_fox[l], w_br_diff=w_br_diff[l], w_br_pool=w_br_pool[l],
                 w_gate=w_gate[l], b_gate=b_gate[l], w_o=w_o[l])
        xp, sp = _layer(xp, c_prompt, p, l, pos_prompt, None)
        past = (gather(cache_fox_k[l]), gather(cache_fox_v[l]), gather(cache_fox_logf[l]),
                gather(cache_diff_k[l]), gather(cache_diff_v[l]), state_pool[l])
        xs, ss = _layer(xs, c_sample, p, l, pos_sample, past)
        st_p.append(sp)
        st_s.append(ss)

    def stack(states, i):
        return jnp.stack([s[i] for s in states], axis=0)

    y_prompt = _rmsnorm(xp, final_norm_g)
    y_sample = _rmsnorm(xs, final_norm_g)
    return (y_prompt, y_sample,
            stack(st_p, 0), stack(st_p, 1), stack(st_p, 2), stack(st_p, 3), stack(st_p, 4), stack(st_p, 5),
            stack(st_s, 0), stack(st_s, 1), stack(st_s, 2), stack(st_s, 3), stack(st_s, 4), stack(st_s, 5))
```

```python
import functools
import math

import jax
import jax.numpy as jnp
from jax import lax
from jax.experimental import pallas as pl
from jax.experimental.pallas import tpu as pltpu

F32 = jnp.float32
BF16 = jnp.bfloat16

LANES_V7X = 128
SUBLANES_V7X = 8
BF16_ROWS_V7X = 16
VMEM_CAP_V7X = 58 * 1024 * 1024

POOL_WINDOWS = (2, 4, 8, 16)
ROPE_THETA = 10000.0
EPS = 1e-6
N_MOD = 9
SAMPLE_ROWS = BF16_ROWS_V7X


def _pick(n, prefs):
    for p in prefs:
        if n % p == 0:
            return p
    return n


def _nbytes(shape, dtype):
    return math.prod(shape) * jnp.dtype(dtype).itemsize


def _params(sem, vmem_estimate):
    limit = min(VMEM_CAP_V7X, int(vmem_estimate * 5 // 4) + (4 << 20))
    return pltpu.CompilerParams(dimension_semantics=sem, vmem_limit_bytes=limit)


def _w_imap(lead, col0, j, i):
    return (*lead, 0, col0 + j)


def _x_imap(j, i):
    return (i, 0)


def _out_imap(j, i):
    return (i, j)


def _mm_kernel(*refs, n_x, n_p, n_e, n_o, x_of, prologue, epilogue):
    x_refs = refs[:n_x]
    w_refs = refs[n_x:n_x + n_p]
    e_refs = refs[n_x + n_p:n_x + n_p + n_e]
    o_refs = refs[n_x + n_p + n_e:n_x + n_p + n_e + n_o]
    wb_refs = refs[n_x + n_p + n_e + n_o:]

    @pl.when(pl.program_id(1) == 0)
    def _cast_weights():
        for w_ref, wb_ref in zip(w_refs, wb_refs):
            wb_ref[...] = w_ref[...].astype(BF16)

    xs = [x_ref[...] for x_ref in x_refs]
    if prologue is not None:
        xs = [prologue(x) for x in xs]
    accs = [jnp.dot(xs[x_of[p]], wb_refs[p][...], preferred_element_type=F32) for p in range(n_p)]
    outs = epilogue(accs, [e_ref[...] for e_ref in e_refs])
    for o_ref, o in zip(o_refs, outs):
        o_ref[...] = o.astype(o_ref.dtype)


def _matmul(xs, products, extras, epilogue, out_dtypes, *, tm, tn, n_tiles, name, prologue=None):
    m = xs[0].shape[0]
    grid = (n_tiles, m // tm)
    in_specs, vmem = [], 0
    for x in xs:
        in_specs.append(pl.BlockSpec((tm, x.shape[1]), _x_imap))
        vmem += 2 * _nbytes((tm, x.shape[1]), x.dtype)
    scratch = []
    for _, w, lead, col0 in products:
        k = w.shape[-2]
        in_specs.append(pl.BlockSpec((None,) * len(lead) + (k, tn), functools.partial(_w_imap, lead, col0)))
        scratch.append(pltpu.VMEM((k, tn), BF16))
        vmem += 2 * _nbytes((k, tn), w.dtype) + _nbytes((k, tn), BF16)
    for arr, block, imap in extras:
        in_specs.append(pl.BlockSpec(block, imap))
        vmem += 2 * _nbytes([b for b in block if b is not None], arr.dtype)
    out_shape = [jax.ShapeDtypeStruct((m, n_tiles * tn), dt) for dt in out_dtypes]
    out_specs = [pl.BlockSpec((tm, tn), _out_imap) for _ in out_dtypes]
    vmem += sum(2 * _nbytes((tm, tn), dt) for dt in out_dtypes) + 4 * len(products) * _nbytes((tm, tn), F32)
    body = functools.partial(
        _mm_kernel, n_x=len(xs), n_p=len(products), n_e=len(extras), n_o=len(out_dtypes),
        x_of=tuple(p[0] for p in products), prologue=prologue, epilogue=epilogue)
    outs = pl.pallas_call(
        body, grid=grid, in_specs=in_specs, out_specs=out_specs, out_shape=out_shape,
        scratch_shapes=scratch, name=name,
        compiler_params=_params(("arbitrary", "arbitrary"), vmem),
    )(*xs, *[p[1] for p in products], *[e[0] for e in extras])
    return outs


def _mod_extra(arr, tn, tiles_per_group):
    _, r, _ = arr.shape
    return (arr, (None, r, tn), lambda j, i: (i // tiles_per_group, 0, j))


def _tile_extra(arr, tm, tn, col0=0):
    return (arr, (tm, tn), lambda j, i: (i, col0 + j))


def _ep_plain(accs, extras):
    return [accs[0]]


def _ep_bias(accs, extras):
    return [accs[0] + extras[0]]


def _ep_swiglu(accs, extras):
    gate, up = accs
    return [gate * jax.nn.sigmoid(gate) * up]


def _ep_sigmoid_bias(accs, extras):
    return [jax.nn.sigmoid(accs[0] + extras[0])]


def _ep_residual(coef, accs, extras):
    x_res, gate = extras
    return [x_res + (coef * gate) * accs[0]]


def _ep_merge(accs, extras):
    out = extras[0].astype(F32) * accs[0]
    for g, a in zip(extras[1:], accs[1:]):
        out = out + g.astype(F32) * a
    return [out]


def _ep_rope(head_dim, accs, extras):
    cos, sin_signed = extras
    acc = accs[0]
    heads = []
    for h in range(acc.shape[1] // head_dim):
        a = acc[:, h * head_dim:(h + 1) * head_dim]
        heads.append(a * cos + pltpu.roll(a, head_dim // 2, 1) * sin_signed)
    return [jnp.concatenate(heads, axis=1) if len(heads) > 1 else heads[0]]


def _silu_bf16(x):
    return (x * jax.nn.sigmoid(x)).astype(BF16)


def _norm_kernel(x_ref, g_ref, *rest, modulated):
    x = x_ref[...]
    y = x * lax.rsqrt(jnp.mean(x * x, axis=-1, keepdims=True) + EPS) * g_ref[...]
    if modulated:
        scale_ref, shift_ref, o_ref = rest
        y = y * (1.0 + scale_ref[...]) + shift_ref[...]
    else:
        (o_ref,) = rest
    o_ref[...] = y.astype(o_ref.dtype)


def _norm(x, g, scale, shift, *, tm, tiles_per_group, out_dtype):
    m, d = x.shape
    in_specs = [pl.BlockSpec((tm, d), lambda i: (i, 0)), pl.BlockSpec((1, d), lambda i: (0, 0))]
    args = [x, g.reshape(1, d)]
    if scale is not None:
        r = scale.shape[1]
        spec = pl.BlockSpec((None, r, d), lambda i: (i // tiles_per_group, 0, 0))
        in_specs += [spec, spec]
        args += [scale, shift]
    vmem = 6 * _nbytes((tm, d), F32)
    return pl.pallas_call(
        functools.partial(_norm_kernel, modulated=scale is not None), name='norm',
        grid=(m // tm,), in_specs=in_specs, out_specs=pl.BlockSpec((tm, d), lambda i: (i, 0)),
        out_shape=jax.ShapeDtypeStruct((m, d), out_dtype),
        compiler_params=_params(("arbitrary",), vmem),
    )(*args)


def _shift_rows(x, k):
    row = lax.broadcasted_iota(jnp.int32, x.shape, 0)
    return jnp.where(row >= k, pltpu.roll(x, k, 0), 0.0)


def _logf_kernel(z_ref, b_ref, logf_ref, cum_ref, cum_t_ref, *, n_heads, seq, cumulative):
    z = z_ref[...] + b_ref[...]
    logf = jnp.minimum(z, 0.0) - jnp.log1p(jnp.exp(-jnp.abs(z)))
    logf_ref[...] = logf[:, :n_heads]
    cum = logf
    if cumulative:
        k = 1
        while k < seq:
            cum = cum + _shift_rows(cum, k)
            k *= 2
    cum_ref[...] = cum
    cum_t_ref[...] = jnp.transpose(cum)[:cum_t_ref.shape[0], :]


def _logf(z, b_pad, *, n_seq, seq, n_heads, cumulative):
    m, lanes = z.shape
    head_rows = BF16_ROWS_V7X
    return pl.pallas_call(
        functools.partial(_logf_kernel, n_heads=n_heads, seq=seq, cumulative=cumulative), name='logf',
        grid=(n_seq,),
        in_specs=[pl.BlockSpec((seq, lanes), lambda b: (b, 0)), pl.BlockSpec((1, lanes), lambda b: (0, 0))],
        out_specs=[pl.BlockSpec((seq, n_heads), lambda b: (b, 0)),
                   pl.BlockSpec((seq, lanes), lambda b: (b, 0)),
                   pl.BlockSpec((None, head_rows, seq), lambda b: (b, 0, 0))],
        out_shape=[jax.ShapeDtypeStruct((m, n_heads), F32), jax.ShapeDtypeStruct((m, lanes), F32),
                   jax.ShapeDtypeStruct((n_seq, head_rows, seq), F32)],
        compiler_params=_params(("arbitrary",), 16 * _nbytes((seq, lanes), F32)),
    )(z, b_pad)


def _causal_mask(s, row0):
    row = row0 + lax.broadcasted_iota(jnp.int32, s.shape, 0)
    col = lax.broadcasted_iota(jnp.int32, s.shape, 1)
    return jnp.where(col <= row, s, -jnp.inf)


def _nt_dot(a, b):
    return lax.dot_general(a, b, (((1,), (1,)), ((), ())), preferred_element_type=F32)


def _fox_kernel(q_ref, k_ref, v_ref, cc_ref, cr_ref, o_ref, kb_ref, vb_ref, *, tq, nq, scale):
    h = pl.program_id(1)
    qi = pl.program_id(2)

    @pl.when(qi == 0)
    def _cast_kv():
        kb_ref[...] = k_ref[...].astype(BF16)
        vb_ref[...] = v_ref[...].astype(BF16)

    q = q_ref[...]
    lane = lax.broadcasted_iota(jnp.int32, cc_ref.shape, 1)
    cum_q = jnp.sum(jnp.where(lane == h, cc_ref[...], 0.0), axis=1, keepdims=True)
    for n in range(nq):
        @pl.when(qi == n)
        def _block(n=n):
            n_keys = (n + 1) * tq
            s = _nt_dot(q, kb_ref[0:n_keys, :]) * scale
            s = s + cum_q - cr_ref[pl.ds(h, 1), 0:n_keys]
            s = _causal_mask(s, n * tq)
            p = jnp.exp(s - jnp.max(s, axis=1, keepdims=True))
            denom = jnp.sum(p, axis=1, keepdims=True)
            o = jnp.dot(p.astype(BF16), vb_ref[0:n_keys, :], preferred_element_type=F32)
            o_ref[...] = (o / denom).astype(o_ref.dtype)


def _fox_attention(q, k, v, cum_cols, cum_rows, *, n_seq, seq, n_heads, head_dim):
    m = q.shape[0]
    tq = _pick(seq, (512, 256, 128))
    nq = seq // tq
    lanes = cum_cols.shape[1]
    vmem = (4 * _nbytes((seq, head_dim), F32) + 2 * _nbytes((seq, head_dim), BF16)
            + 6 * _nbytes((tq, seq), F32) + 4 * _nbytes((tq, lanes), F32))
    return pl.pallas_call(
        functools.partial(_fox_kernel, tq=tq, nq=nq, scale=head_dim ** -0.5), name='fox_attn',
        grid=(n_seq, n_heads, nq),
        in_specs=[pl.BlockSpec((tq, head_dim), lambda b, h, i: (b * nq + i, h)),
                  pl.BlockSpec((seq, head_dim), lambda b, h, i: (b, h)),
                  pl.BlockSpec((seq, head_dim), lambda b, h, i: (b, h)),
                  pl.BlockSpec((tq, lanes), lambda b, h, i: (b * nq + i, 0)),
                  pl.BlockSpec((None, cum_rows.shape[1], seq), lambda b, h, i: (b, 0, 0))],
        out_specs=pl.BlockSpec((tq, head_dim), lambda b, h, i: (b * nq + i, h)),
        out_shape=jax.ShapeDtypeStruct((m, n_heads * head_dim), BF16),
        scratch_shapes=[pltpu.VMEM((seq, head_dim), BF16), pltpu.VMEM((seq, head_dim), BF16)],
        compiler_params=_params(("arbitrary", "arbitrary", "arbitrary"), vmem),
    )(q, k, v, cum_cols, cum_rows)


def _diff_lambda(lq, lam_init):
    a = jnp.sum(lq[0:1, :] * lq[1:2, :], axis=1, keepdims=True)
    b = jnp.sum(lq[2:3, :] * lq[3:4, :], axis=1, keepdims=True)
    return jnp.exp(a) - jnp.exp(b) + lam_init


def _sub_norm(o, g, lam_init):
    return o * lax.rsqrt(jnp.mean(o * o, axis=-1, keepdims=True) + EPS) * g * (1.0 - lam_init)


def _diff_kernel(q_ref, k_ref, v_ref, lq_ref, g_ref, o_ref, kb_ref, vb_ref, *, tq, nq, scale, head_dim, lam_init):
    qi = pl.program_id(2)

    @pl.when(qi == 0)
    def _cast_kv():
        kb_ref[...] = k_ref[...].astype(BF16)
        vb_ref[...] = v_ref[...].astype(BF16)

    q = q_ref[...]
    lam = _diff_lambda(lq_ref[...], lam_init)
    for n in range(nq):
        @pl.when(qi == n)
        def _block(n=n):
            n_keys = (n + 1) * tq
            probs = []
            for c in range(2):
                cols = slice(c * head_dim, (c + 1) * head_dim)
                s = _causal_mask(_nt_dot(q[:, cols], kb_ref[0:n_keys, cols]) * scale, n * tq)
                p = jnp.exp(s - jnp.max(s, axis=1, keepdims=True))
                probs.append(p / jnp.sum(p, axis=1, keepdims=True))
            a = probs[0] - lam * probs[1]
            o = jnp.dot(a.astype(BF16), vb_ref[0:n_keys, :], preferred_element_type=F32)
            o_ref[...] = _sub_norm(o, g_ref[...], lam_init).astype(o_ref.dtype)


def _diff_attention(q, k, v, lambda_qk, subln_g, layer, *, n_seq, seq, n_heads, head_dim, lam_init):
    m = q.shape[0]
    tq = _pick(seq, (512, 256, 128))
    nq = seq // tq
    width = 2 * head_dim
    vmem = (4 * _nbytes((seq, width), F32) + 2 * _nbytes((seq, width), BF16) + 8 * _nbytes((tq, seq), F32))
    return pl.pallas_call(
        functools.partial(_diff_kernel, tq=tq, nq=nq, scale=head_dim ** -0.5, head_dim=head_dim,
                          lam_init=lam_init), name='diff_attn',
        grid=(n_seq, n_heads, nq),
        in_specs=[pl.BlockSpec((tq, width), lambda b, h, i: (b * nq + i, h)),
                  pl.BlockSpec((seq, width), lambda b, h, i: (b, h)),
                  pl.BlockSpec((seq, width), lambda b, h, i: (b, h)),
                  pl.BlockSpec((None, 4, head_dim), lambda b, h, i: (layer, 0, 0)),
                  pl.BlockSpec((None, 1, width), lambda b, h, i: (layer, 0, 0))],
        out_specs=pl.BlockSpec((tq, width), lambda b, h, i: (b * nq + i, h)),
        out_shape=jax.ShapeDtypeStruct((m, n_heads * width), BF16),
        scratch_shapes=[pltpu.VMEM((seq, width), BF16), pltpu.VMEM((seq, width), BF16)],
        compiler_params=_params(("arbitrary", "arbitrary", "arbitrary"), vmem),
    )(q, k, v, lambda_qk, subln_g.reshape(subln_g.shape[0], 1, width))


def _pool_kernel(u_ref, w_ref, ps_ref, o_ref, *, seq, group_w):
    row = lax.broadcasted_iota(jnp.int32, (seq, 1), 0)
    for g, window in enumerate(POOL_WINDOWS):
        cols = slice(g * group_w, (g + 1) * group_w)
        u = u_ref[:, cols]
        total = u
        k = 1
        while k < window:
            total = total + _shift_rows(total, k)
            k *= 2
        count = jnp.minimum(row + 1, window).astype(F32)
        pooled = total / count - u
        y = jnp.dot(pooled.astype(BF16), w_ref[g].astype(BF16), preferred_element_type=F32)
        o_ref[:, cols] = (y * ps_ref[:, cols]).astype(o_ref.dtype)


def _pool(u, w_pool, pool_scale, layer):
    n_seq, seq, width = u.shape
    group_w = width // len(POOL_WINDOWS)
    return pl.pallas_call(
        functools.partial(_pool_kernel, seq=seq, group_w=group_w), name='pool',
        grid=(n_seq,),
        in_specs=[pl.BlockSpec((None, seq, width), lambda b: (b, 0, 0)),
                  pl.BlockSpec((None,) + w_pool.shape[1:], lambda b: (layer, 0, 0, 0)),
                  pl.BlockSpec((None, 1, width), lambda b: (layer, 0, 0))],
        out_specs=pl.BlockSpec((None, seq, width), lambda b: (b, 0, 0)),
        out_shape=jax.ShapeDtypeStruct((n_seq, seq, width), BF16),
        compiler_params=_params(("arbitrary",), 5 * _nbytes((seq, width), F32)),
    )(u, w_pool, pool_scale.reshape(pool_scale.shape[0], 1, width))


def _split3(x):
    hi = x.astype(BF16)
    r = x - hi.astype(F32)
    mid = r.astype(BF16)
    lo = (r - mid.astype(F32)).astype(BF16)
    return hi, mid, lo


def _row_select(stack_ref, n, rows):
    row = lax.broadcasted_iota(jnp.int32, stack_ref.shape[1:], 0)
    out = jnp.zeros(stack_ref.shape[1:], F32)
    for h in range(n):
        out = jnp.where(row == h, stack_ref[h], out)
    return out


def _decode_kernel(pt_ref, qf_ref, kfn_ref, vfn_ref, lfn_ref, qd_ref, kdn_ref, vdn_ref, lq_ref, g_ref,
                   ck_ref, cv_ref, clf_ref, cdk_ref, cdv_ref, of_ref, od_ref,
                   qbf_ref, qbd_ref, mf_ref, lf_ref, accf_ref, cf_ref, md_ref, ld_ref, accd_ref,
                   *, n_fox, n_diff, head_dim, scale, lam_init):
    del pt_ref
    p = pl.program_id(1)
    rows = qf_ref.shape[0]
    row_id = lax.broadcasted_iota(jnp.int32, (rows, head_dim), 0)

    @pl.when(p == 0)
    def _init():
        qf = qf_ref[...]
        qd = qd_ref[...]
        for h in range(n_fox):
            qbf_ref[h] = jnp.where(row_id == h, qf, 0.0).astype(BF16)
            accf_ref[h] = jnp.broadcast_to(vfn_ref[h:h + 1, :], (rows, head_dim))
        for r in range(2 * n_diff):
            qbd_ref[r] = jnp.where(row_id == r, qd, 0.0).astype(BF16)
        for h in range(n_diff):
            accd_ref[h] = jnp.broadcast_to(vdn_ref[h:h + 1, :], (rows, 2 * head_dim))
        mf_ref[...] = jnp.sum(qf.astype(BF16).astype(F32) * kfn_ref[...].astype(BF16).astype(F32),
                              axis=1, keepdims=True) * scale
        md_ref[...] = jnp.sum(qd.astype(BF16).astype(F32) * kdn_ref[...].astype(BF16).astype(F32),
                              axis=1, keepdims=True) * scale
        lf_ref[...] = jnp.ones_like(lf_ref)
        ld_ref[...] = jnp.ones_like(ld_ref)
        cf_ref[...] = lfn_ref[...]

    logf_t = clf_ref[...]
    page = logf_t.shape[1]
    later = (lax.broadcasted_iota(jnp.int32, (page, page), 0)
             > lax.broadcasted_iota(jnp.int32, (page, page), 1)).astype(BF16)
    suffix = sum(jnp.dot(t, later, preferred_element_type=F32) for t in _split3(logf_t))
    bias = cf_ref[...] + suffix
    cf_ref[...] = cf_ref[...] + jnp.sum(logf_t, axis=1, keepdims=True)

    s = jnp.zeros((rows, page), F32)
    for h in range(n_fox):
        s = s + _nt_dot(qbf_ref[h], ck_ref[:, h, :].astype(BF16))
    s = s * scale + bias
    m_new = jnp.maximum(mf_ref[...], jnp.max(s, axis=1, keepdims=True))
    alpha = jnp.exp(mf_ref[...] - m_new)
    prob = jnp.exp(s - m_new)
    lf_ref[...] = alpha * lf_ref[...] + jnp.sum(prob, axis=1, keepdims=True)
    mf_ref[...] = m_new
    prob = prob.astype(BF16)
    for h in range(n_fox):
        accf_ref[h] = alpha * accf_ref[h] + jnp.dot(prob, cv_ref[:, h, :].astype(BF16),
                                                    preferred_element_type=F32)

    s = jnp.zeros((rows, page), F32)
    for r in range(2 * n_diff):
        s = s + _nt_dot(qbd_ref[r], cdk_ref[:, r, :].astype(BF16))
    s = s * scale
    m_new = jnp.maximum(md_ref[...], jnp.max(s, axis=1, keepdims=True))
    alpha = jnp.exp(md_ref[...] - m_new)
    prob = jnp.exp(s - m_new)
    ld_ref[...] = alpha * ld_ref[...] + jnp.sum(prob, axis=1, keepdims=True)
    md_ref[...] = m_new
    prob = prob.astype(BF16)
    for h in range(n_diff):
        accd_ref[h] = alpha * accd_ref[h] + jnp.dot(prob, cdv_ref[:, h, :].astype(BF16),
                                                    preferred_element_type=F32)

    @pl.when(p == pl.num_programs(1) - 1)
    def _finish():
        of_ref[...] = _row_select(accf_ref, n_fox, rows) / lf_ref[...]
        lam = _diff_lambda(lq_ref[...], lam_init)
        od_ref[...] = jnp.zeros_like(od_ref)
        for h in range(n_diff):
            a0 = accd_ref[h][2 * h:2 * h + 1, :] / ld_ref[2 * h:2 * h + 1, :]
            a1 = accd_ref[h][2 * h + 1:2 * h + 2, :] / ld_ref[2 * h + 1:2 * h + 2, :]
            od_ref[h:h + 1, :] = _sub_norm(a0 - lam * a1, g_ref[...], lam_init)


def _decode_attention(page_table, qf, kfn, vfn, lfn, qd, kdn, vdn, lambda_qk, subln_g,
                      cache_fox_k, cache_fox_v, logf_t, cache_diff_k, cache_diff_v, layer,
                      *, n_fox, n_diff, head_dim, lam_init):
    n_seq, n_pages = page_table.shape
    rows = qf.shape[1]
    page = cache_fox_k.shape[2]
    width = 2 * head_dim
    vrows = vdn.shape[1]

    def per_seq(shape):
        return pl.BlockSpec((None,) + shape, lambda b, p, pt: (b, 0, 0))

    def paged(shape):
        return pl.BlockSpec((None, None) + shape,
                            lambda b, p, pt: (layer, pt[b, n_pages - 1 - p]) + (0,) * len(shape))

    in_specs = [per_seq((rows, head_dim)), per_seq((rows, head_dim)), per_seq((rows, head_dim)),
                per_seq((rows, 1)),
                per_seq((rows, head_dim)), per_seq((rows, head_dim)), per_seq((vrows, width)),
                pl.BlockSpec((None, 4, head_dim), lambda b, p, pt: (layer, 0, 0)),
                pl.BlockSpec((None, 1, width), lambda b, p, pt: (layer, 0, 0)),
                paged((page, n_fox, head_dim)), paged((page, n_fox, head_dim)), paged((rows, page)),
                paged((page, 2 * n_diff, head_dim)), paged((page, n_diff, width))]
    scratch = [pltpu.VMEM((n_fox, rows, head_dim), BF16), pltpu.VMEM((2 * n_diff, rows, head_dim), BF16),
               pltpu.VMEM((rows, 1), F32), pltpu.VMEM((rows, 1), F32),
               pltpu.VMEM((n_fox, rows, head_dim), F32), pltpu.VMEM((rows, 1), F32),
               pltpu.VMEM((rows, 1), F32), pltpu.VMEM((rows, 1), F32),
               pltpu.VMEM((n_diff, rows, width), F32)]
    padded_heads = -(-n_fox // SUBLANES_V7X) * SUBLANES_V7X
    vmem = 2 * 3 * _nbytes((page, padded_heads, head_dim), F32) + 2 * _nbytes((page, SUBLANES_V7X, width), F32)
    vmem += 4 << 20
    grid_spec = pltpu.PrefetchScalarGridSpec(
        num_scalar_prefetch=1, grid=(n_seq, n_pages), in_specs=in_specs,
        out_specs=[pl.BlockSpec((None, rows, head_dim), lambda b, p, pt: (b, 0, 0)),
                   pl.BlockSpec((None, vrows, width), lambda b, p, pt: (b, 0, 0))],
        scratch_shapes=scratch)
    return pl.pallas_call(
        functools.partial(_decode_kernel, n_fox=n_fox, n_diff=n_diff, head_dim=head_dim,
                          scale=head_dim ** -0.5, lam_init=lam_init),
        grid_spec=grid_spec, name='decode_attn',
        out_shape=[jax.ShapeDtypeStruct((n_seq, rows, head_dim), F32),
                   jax.ShapeDtypeStruct((n_seq, vrows, width), F32)],
        compiler_params=_params(("arbitrary", "arbitrary"), vmem),
    )(page_table, qf, kfn, vfn, lfn, qd, kdn, vdn, lambda_qk,
      subln_g.reshape(subln_g.shape[0], 1, width),
      cache_fox_k, cache_fox_v, logf_t, cache_diff_k, cache_diff_v)


def _rope_tables(pos, head_dim):
    half = head_dim // 2
    inv_freq = ROPE_THETA ** (-jnp.arange(half, dtype=F32) / half)
    ang = pos.astype(F32)[:, None] * inv_freq[None, :]
    cos, sin = jnp.cos(ang), jnp.sin(ang)
    return jnp.concatenate([cos, cos], axis=-1), jnp.concatenate([-sin, sin], axis=-1)


class _Stream:
    def __init__(self, n_rows, n_groups, seq):
        self.m = n_rows
        self.tm = _pick(n_rows, (1024, 512, 256, 128))
        self.tm_wide_k = _pick(n_rows, (512, 256, 128))
        self.tm_norm = _pick(n_rows, (256, 128))
        self.n_groups = n_groups
        self.seq = seq

    def tiles_per_group(self, tm):
        return max(1, self.m // self.n_groups // tm)


def _layer(st, x, mods, w, layer, rope, dims, attention):
    d = x.shape[1]
    tm, tpg = st.tm, st.tiles_per_group(st.tm)
    tmn, tpgn = st.tm_norm, st.tiles_per_group(st.tm_norm)
    d_ff = w['w_ffn_out'].shape[2]
    n_fox, n_diff, head_dim, pool_w = dims['n_fox'], dims['n_diff'], dims['head_dim'], dims['pool_w']
    fox_w = n_fox * head_dim

    def swiglu_ffn(x, which, norm_idx):
        h = _norm(x, w['norm_g'][layer, norm_idx], mods[3 * norm_idx + 1], mods[3 * norm_idx],
                  tm=tmn, tiles_per_group=tpgn, out_dtype=BF16)
        tn = _pick(d_ff, (256, 128))
        lead = (layer, which)
        (hid,) = _matmul([h], [(0, w['w_ffn_in'], lead, 0), (0, w['w_ffn_in'], lead, d_ff // tn)], [],
                         _ep_swiglu, [BF16], tm=tm, tn=tn, n_tiles=d_ff // tn, name='ffn_in')
        tm2 = st.tm_wide_k
        tn2 = _pick(d, (256, 128))
        (x_new,) = _matmul([hid], [(0, w['w_ffn_out'], lead, 0)],
                           [_tile_extra(x, tm2, tn2), _mod_extra(mods[3 * norm_idx + 2], tn2, st.tiles_per_group(tm2))],
                           functools.partial(_ep_residual, 0.5), [F32], tm=tm2, tn=tn2, n_tiles=d // tn2,
                           name='ffn_out')
        return x_new

    x = swiglu_ffn(x, 0, 0)

    h = _norm(x, w['norm_g'][layer, 1], mods[4], mods[3], tm=tmn, tiles_per_group=tpgn, out_dtype=BF16)
    tn = _pick(fox_w, (512, 256, 128))
    nt = fox_w // tn
    w_in = w['w_in']
    off_ff = 3 * fox_w
    off_d = off_ff + n_fox

    def proj(weight, lead, col0, n_tiles, out_dtype, rope_it=False):
        if rope_it:
            cos, sin_signed = rope
            rows = cos.shape[0]
            extras = [(cos, (tm, head_dim), lambda j, i: (i % (rows // tm), 0)),
                      (sin_signed, (tm, head_dim), lambda j, i: (i % (rows // tm), 0))]
            ep = functools.partial(_ep_rope, head_dim)
        else:
            extras, ep = [], _ep_plain
        (out,) = _matmul([h], [(0, weight, lead, col0)], extras, ep, [out_dtype], tm=tm, tn=tn, n_tiles=n_tiles,
                         name='proj_rope' if rope_it else 'proj')
        return out

    fq = proj(w_in, (layer,), 0, nt, BF16)
    fk = proj(w_in, (layer,), nt, nt, F32)
    fv = proj(w_in, (layer,), 2 * nt, nt, F32)
    w_d = w['w_in_diff']
    dq = proj(w_d, (layer,), 0, nt, BF16, rope_it=True)
    dk = proj(w_d, (layer,), nt, nt, F32, rope_it=True)
    dv = proj(w_d, (layer,), 2 * nt, nt, F32)
    u = proj(w_d, (layer,), 3 * nt, pool_w // tn, F32)
    (z,) = _matmul([h], [(0, w['w_in_forget'], (layer,), 0)], [], _ep_plain, [F32],
                   tm=tm, tn=LANES_V7X, n_tiles=1, name='proj_forget')

    y_fox, y_diff, y_pool, logf, u_state = attention(fq, fk, fv, z, dq, dk, dv, u)

    tng = _pick(d, (512, 256, 128))
    ntg = d // tng
    (gates,) = _matmul([h], [(0, w['w_gate'], (layer,), 0)],
                       [(w['b_gate'].reshape(-1, 1, 3 * d), (None, 1, tng), lambda j, i: (layer, 0, j))],
                       _ep_sigmoid_bias, [BF16], tm=tm, tn=tng, n_tiles=3 * ntg, name='gates')
    (merged,) = _matmul([y_fox, y_diff, y_pool],
                        [(0, w['w_br_fox'], (layer,), 0), (1, w['w_br_diff'], (layer,), 0),
                         (2, w['w_br_pool'], (layer,), 0)],
                        [_tile_extra(gates, tm, tng, b * ntg) for b in range(3)],
                        _ep_merge, [BF16], tm=tm, tn=tng, n_tiles=ntg, name='merge')
    (x,) = _matmul([merged], [(0, w['w_o'], (layer,), 0)],
                   [_tile_extra(x, tm, tng), _mod_extra(mods[5], tng, tpg)],
                   functools.partial(_ep_residual, 1.0), [F32], tm=tm, tn=tng, n_tiles=ntg, name='w_o')

    x = swiglu_ffn(x, 1, 2)
    return x, (fk, fv, logf, dk, dv, u_state)


def kernel(x_prompt, x_sample, c_prompt, c_sample, page_table, cache_fox_k, cache_fox_v, cache_fox_logf, cache_diff_k, cache_diff_v, state_pool, norm_g, w_ada, b_ada, w_ffn_in, w_ffn_out, w_in, b_forget, lambda_qk, subln_g, w_pool, pool_scale, w_br_fox, w_br_diff, w_br_pool, w_gate, b_gate, w_o, final_norm_g):
    n_seq, seq, d = x_prompt.shape
    n_dec, dec_seq, _ = x_sample.shape
    assert dec_seq == 1 and n_dec <= SAMPLE_ROWS
    depth = norm_g.shape[0]
    n_fox, head_dim = cache_fox_k.shape[3], cache_fox_k.shape[4]
    n_diff = cache_diff_v.shape[3]
    fox_w = n_fox * head_dim
    pool_w = w_pool.shape[1] * w_pool.shape[2]
    pool_state = state_pool.shape[2]
    n_pages, page = page_table.shape[1], cache_fox_k.shape[2]
    past_len = n_pages * page
    dims = dict(n_fox=n_fox, n_diff=n_diff, head_dim=head_dim, pool_w=pool_w)
    rows = SAMPLE_ROWS
    assert n_fox <= rows and 2 * n_diff <= rows and n_fox <= LANES_V7X

    off_ff = 3 * fox_w
    off_d = off_ff + n_fox
    weights = dict(norm_g=norm_g, w_ffn_in=w_ffn_in, w_ffn_out=w_ffn_out, w_in=w_in, w_gate=w_gate,
                   b_gate=b_gate, w_br_fox=w_br_fox, w_br_diff=w_br_diff, w_br_pool=w_br_pool, w_o=w_o,
                   w_in_diff=w_in[:, :, off_d:],
                   w_in_forget=jnp.pad(w_in[:, :, off_ff:off_d], ((0, 0), (0, 0), (0, LANES_V7X - n_fox))))
    b_forget_pad = jnp.pad(b_forget, ((0, 0), (0, LANES_V7X - n_fox)))

    n_c = n_seq + n_dec
    c_rows = -(-n_c // BF16_ROWS_V7X) * BF16_ROWS_V7X
    c_all = jnp.pad(jnp.concatenate([c_prompt, c_sample], axis=0), ((0, c_rows - n_c), (0, 0)))
    tn_ada = _pick(N_MOD * d, (1024, 512, 256, 128))
    mods_p, mods_s = [], []
    for l in range(depth):
        (mod,) = _matmul([c_all], [(0, w_ada, (l,), 0)],
                         [(b_ada.reshape(depth, 1, N_MOD * d), (None, 1, tn_ada), lambda j, i, l=l: (l, 0, j))],
                         _ep_bias, [F32], tm=c_rows, tn=tn_ada, n_tiles=N_MOD * d // tn_ada, name='ada',
                         prologue=_silu_bf16)
        mod = mod.reshape(c_rows, N_MOD, d)
        mods_p.append([mod[:n_seq, k][:, None, :] for k in range(N_MOD)])
        mods_s.append([jnp.pad(mod[n_seq:n_c, k], ((0, rows - n_dec), (0, 0)))[None] for k in range(N_MOD)])

    st_p = _Stream(n_seq * seq, n_seq, seq)
    st_s = _Stream(rows, 1, 1)
    rope_p = _rope_tables(jnp.arange(seq, dtype=jnp.int32), head_dim)
    rope_s = tuple(jnp.broadcast_to(t, (rows, head_dim))
                   for t in _rope_tables(jnp.full((1,), past_len, jnp.int32), head_dim))
    logf_t = jnp.pad(jnp.swapaxes(cache_fox_logf, 2, 3), ((0, 0), (0, 0), (0, rows - n_fox), (0, 0)))

    def attention_prompt(l, fq, fk, fv, z, dq, dk, dv, u):
        logf, cum_cols, cum_rows = _logf(z, b_forget_pad[l:l + 1], n_seq=n_seq, seq=seq, n_heads=n_fox,
                                         cumulative=True)
        y_fox = _fox_attention(fq, fk, fv, cum_cols, cum_rows, n_seq=n_seq, seq=seq, n_heads=n_fox,
                               head_dim=head_dim)
        lam_init = 0.8 - 0.6 * math.exp(-0.3 * l)
        y_diff = _diff_attention(dq, dk, dv, lambda_qk, subln_g, l, n_seq=n_seq, seq=seq, n_heads=n_diff,
                                 head_dim=head_dim, lam_init=lam_init)
        u3 = u.reshape(n_seq, seq, pool_w)
        y_pool = _pool(u3, w_pool, pool_scale, l).reshape(n_seq * seq, pool_w)
        return y_fox, y_diff, y_pool, logf, u3[:, seq - pool_state:]

    def attention_sample(l, fq, fk, fv, z, dq, dk, dv, u):
        logf, cum_cols, _ = _logf(z, b_forget_pad[l:l + 1], n_seq=1, seq=rows, n_heads=n_fox, cumulative=False)

        def heads(a, n, width):
            a = a[:n_dec].astype(F32).reshape(n_dec, n, width)
            return jnp.pad(a, ((0, 0), (0, -(-n // SUBLANES_V7X) * SUBLANES_V7X - n), (0, 0)))

        def heads16(a, n):
            a = heads(a, n, head_dim)
            return jnp.pad(a, ((0, 0), (0, rows - a.shape[1]), (0, 0)))

        lfn = jnp.pad(cum_cols[:n_dec, :n_fox], ((0, 0), (0, rows - n_fox)))[:, :, None]
        lam_init = 0.8 - 0.6 * math.exp(-0.3 * l)
        o_fox, o_diff = _decode_attention(
            page_table, heads16(fq, n_fox), heads16(fk, n_fox), heads16(fv, n_fox), lfn,
            heads16(dq, 2 * n_diff), heads16(dk, 2 * n_diff), heads(dv, n_diff, 2 * head_dim),
            lambda_qk, subln_g, cache_fox_k, cache_fox_v, logf_t, cache_diff_k, cache_diff_v, l,
            n_fox=n_fox, n_diff=n_diff, head_dim=head_dim, lam_init=lam_init)

        def unheads(o, n):
            o = o[:, :n].reshape(n_dec, -1)
            return jnp.pad(o, ((0, rows - n_dec), (0, 0))).astype(BF16)

        u_ext = jnp.concatenate([state_pool[l], u[:n_dec, None, :]], axis=1)
        y_pool = _pool(u_ext, w_pool, pool_scale, l)[:, -1]
        y_pool = jnp.pad(y_pool, ((0, rows - n_dec), (0, 0)))
        return unheads(o_fox, n_fox), unheads(o_diff, n_diff), y_pool, logf, u_ext[:, 1:]

    xp = x_prompt.reshape(n_seq * seq, d)
    xs = jnp.pad(x_sample.reshape(n_dec, d), ((0, rows - n_dec), (0, 0)))
    states_p, states_s = [], []
    for l in range(depth):
        xp, sp = _layer(st_p, xp, mods_p[l], weights, l, rope_p, dims, functools.partial(attention_prompt, l))
        xs, ss = _layer(st_s, xs, mods_s[l], weights, l, rope_s, dims, functools.partial(attention_sample, l))
        states_p.append(sp)
        states_s.append(ss)

    y_prompt = _norm(xp, final_norm_g, None, None, tm=st_p.tm_norm, tiles_per_group=1, out_dtype=F32)
    y_sample = _norm(xs, final_norm_g, None, None, tm=st_s.tm_norm, tiles_per_group=1, out_dtype=F32)

    def stack_p(i, shape):
        return jnp.stack([s[i].reshape(shape) for s in states_p], axis=0)

    def stack_s(i, shape):
        return jnp.stack([s[i][:n_dec].reshape(shape) for s in states_s], axis=0)

    return (y_prompt.reshape(n_seq, seq, d), y_sample[:n_dec].reshape(n_dec, 1, d),
            stack_p(0, (n_seq, seq, n_fox, head_dim)), stack_p(1, (n_seq, seq, n_fox, head_dim)),
            stack_p(2, (n_seq, seq, n_fox)),
            stack_p(3, (n_seq, seq, 2 * n_diff, head_dim)), stack_p(4, (n_seq, seq, n_diff, 2 * head_dim)),
            jnp.stack([s[5] for s in states_p], axis=0),
            stack_s(0, (n_dec, 1, n_fox, head_dim)), stack_s(1, (n_dec, 1, n_fox, head_dim)),
            stack_s(2, (n_dec, 1, n_fox)),
            stack_s(3, (n_dec, 1, 2 * n_diff, head_dim)), stack_s(4, (n_dec, 1, n_diff, 2 * head_dim)),
            jnp.stack([s[5] for s in states_s], axis=0))
```

```python
import functools
import math

import jax
import jax.numpy as jnp
from jax import lax
from jax.experimental import pallas as pl
from jax.experimental.pallas import tpu as pltpu

F32 = jnp.float32
BF16 = jnp.bfloat16

LANES_V7X = 128
SUBLANES_V7X = 8
BF16_ROWS_V7X = 16
VMEM_CAP_V7X = 58 * 1024 * 1024

POOL_WINDOWS = (2, 4, 8, 16)
ROPE_THETA = 10000.0
EPS = 1e-6
N_MOD = 9
SAMPLE_ROWS = BF16_ROWS_V7X


def _pick(n, prefs):
    for p in prefs:
        if n % p == 0:
            return p
    return n


def _nbytes(shape, dtype):
    return math.prod(shape) * jnp.dtype(dtype).itemsize


def _params(sem, vmem_estimate):
    limit = min(VMEM_CAP_V7X, int(vmem_estimate * 5 // 4) + (4 << 20))
    return pltpu.CompilerParams(dimension_semantics=sem, vmem_limit_bytes=limit)


def _w_imap(lead, col0, j, i):
    return (*lead, 0, col0 + j)


def _x_imap(j, i):
    return (i, 0)


def _out_imap(j, i):
    return (i, j)


def _mm_kernel(*refs, n_x, n_p, n_e, n_o, n_alias, x_of, prologue, epilogue, head_w):
    x_refs = refs[:n_x]
    w_refs = refs[n_x:n_x + n_p]
    e_refs = refs[n_x + n_p:n_x + n_p + n_e]
    n_in = n_x + n_p + n_e + n_alias
    o_refs = refs[n_in:n_in + n_o]
    wb_refs = refs[n_in + n_o:]

    @pl.when(pl.program_id(1) == 0)
    def _cast_weights():
        for w_ref, wb_ref in zip(w_refs, wb_refs):
            wb_ref[...] = w_ref[...].astype(BF16)

    xs = [x_ref[...] for x_ref in x_refs]
    if prologue is not None:
        xs = [prologue(x) for x in xs]
    accs = [jnp.dot(xs[x_of[p]], wb_refs[p][...], preferred_element_type=F32) for p in range(n_p)]
    outs = epilogue(accs, [e_ref[...] for e_ref in e_refs])
    for o_ref, o in zip(o_refs, outs):
        if head_w is None:
            o_ref[...] = o.astype(o_ref.dtype)
        else:
            for h in range(o_ref.shape[0]):
                o_ref[h] = o[:, h * head_w:(h + 1) * head_w].astype(o_ref.dtype)


def _matmul(xs, products, extras, epilogue, out_dtypes, *, tm, tn, n_tiles, name, prologue=None, head_major=None,
            single_buffer_weights=False):
    m = xs[0].shape[0]
    grid = (n_tiles, m // tm)
    in_specs, vmem = [], 0
    for x in xs:
        in_specs.append(pl.BlockSpec((tm, x.shape[1]), _x_imap))
        vmem += 2 * _nbytes((tm, x.shape[1]), x.dtype)
    scratch = []
    for _, w, lead, col0 in products:
        k = w.shape[-2]
        mode = dict(pipeline_mode=pl.Buffered(1)) if single_buffer_weights else {}
        in_specs.append(pl.BlockSpec((None,) * len(lead) + (k, tn), functools.partial(_w_imap, lead, col0),
                                     **mode))
        scratch.append(pltpu.VMEM((k, tn), BF16))
        vmem += (1 if single_buffer_weights else 2) * _nbytes((k, tn), w.dtype) + _nbytes((k, tn), BF16)
    for arr, block, imap in extras:
        in_specs.append(pl.BlockSpec(block, imap))
        vmem += 2 * _nbytes([b for b in block if b is not None], arr.dtype)
    operands = [*xs, *[p[1] for p in products], *[e[0] for e in extras]]
    aliases, head_w = {}, None
    if head_major is None:
        out_shape = [jax.ShapeDtypeStruct((m, n_tiles * tn), dt) for dt in out_dtypes]
        out_specs = [pl.BlockSpec((tm, tn), _out_imap) for _ in out_dtypes]
    else:
        buf, layer, seq_tiles = head_major
        head_w = buf.shape[-1]
        out_shape = [jax.ShapeDtypeStruct(buf.shape, buf.dtype)]
        out_specs = [pl.BlockSpec((None, None, tn // head_w, tm, head_w),
                                  lambda j, i: (layer, i // seq_tiles, j, i % seq_tiles, 0))]
        in_specs.append(pl.BlockSpec(memory_space=pl.ANY))
        aliases = {len(operands): 0}
        operands.append(buf)
    vmem += sum(2 * _nbytes((tm, tn), dt) for dt in out_dtypes) + 4 * len(products) * _nbytes((tm, tn), F32)
    body = functools.partial(
        _mm_kernel, n_x=len(xs), n_p=len(products), n_e=len(extras), n_o=len(out_dtypes),
        n_alias=len(aliases), x_of=tuple(p[0] for p in products), prologue=prologue, epilogue=epilogue,
        head_w=head_w)
    outs = pl.pallas_call(
        body, grid=grid, in_specs=in_specs, out_specs=out_specs, out_shape=out_shape,
        scratch_shapes=scratch, name=name, input_output_aliases=aliases,
        compiler_params=_params(("arbitrary", "arbitrary"), vmem),
    )(*operands)
    return outs


def _mod_extra(arr, tn, tiles_per_group):
    _, r, _ = arr.shape
    return (arr, (None, r, tn), lambda j, i: (i // tiles_per_group, 0, j))


def _tile_extra(arr, tm, tn, col0=0):
    return (arr, (tm, tn), lambda j, i: (i, col0 + j))


def _ep_plain(accs, extras):
    return [accs[0]]


def _ep_bias(accs, extras):
    return [accs[0] + extras[0]]


def _ep_swiglu(accs, extras):
    gate, up = accs
    return [gate * jax.nn.sigmoid(gate) * up]


def _ep_sigmoid_bias(accs, extras):
    return [jax.nn.sigmoid(accs[0] + extras[0])]


def _ep_residual(coef, accs, extras):
    x_res, gate = extras
    return [x_res + (coef * gate) * accs[0]]


def _ep_merge(accs, extras):
    out = extras[0].astype(F32) * accs[0]
    for g, a in zip(extras[1:], accs[1:]):
        out = out + g.astype(F32) * a
    return [out]


def _ep_rope(head_dim, accs, extras):
    cos, sin_signed = extras
    acc = accs[0]
    heads = []
    for h in range(acc.shape[1] // head_dim):
        a = acc[:, h * head_dim:(h + 1) * head_dim]
        heads.append(a * cos + pltpu.roll(a, head_dim // 2, 1) * sin_signed)
    return [jnp.concatenate(heads, axis=1) if len(heads) > 1 else heads[0]]


def _silu_bf16(x):
    return (x * jax.nn.sigmoid(x)).astype(BF16)


def _norm_kernel(x_ref, g_ref, *rest, modulated):
    x = x_ref[...]
    y = x * lax.rsqrt(jnp.mean(x * x, axis=-1, keepdims=True) + EPS) * g_ref[...]
    if modulated:
        scale_ref, shift_ref, o_ref = rest
        y = y * (1.0 + scale_ref[...]) + shift_ref[...]
    else:
        (o_ref,) = rest
    o_ref[...] = y.astype(o_ref.dtype)


def _norm(x, g, scale, shift, *, tm, tiles_per_group, out_dtype):
    m, d = x.shape
    in_specs = [pl.BlockSpec((tm, d), lambda i: (i, 0)), pl.BlockSpec((1, d), lambda i: (0, 0))]
    args = [x, g.reshape(1, d)]
    if scale is not None:
        r = scale.shape[1]
        spec = pl.BlockSpec((None, r, d), lambda i: (i // tiles_per_group, 0, 0))
        in_specs += [spec, spec]
        args += [scale, shift]
    vmem = 6 * _nbytes((tm, d), F32)
    return pl.pallas_call(
        functools.partial(_norm_kernel, modulated=scale is not None), name='norm',
        grid=(m // tm,), in_specs=in_specs, out_specs=pl.BlockSpec((tm, d), lambda i: (i, 0)),
        out_shape=jax.ShapeDtypeStruct((m, d), out_dtype),
        compiler_params=_params(("arbitrary",), vmem),
    )(*args)


def _shift_rows(x, k):
    row = lax.broadcasted_iota(jnp.int32, x.shape, 0)
    return jnp.where(row >= k, pltpu.roll(x, k, 0), 0.0)


def _logf_kernel(z_ref, b_ref, logf_ref, cum_ref, cum_t_ref, *, n_heads, seq, cumulative):
    z = z_ref[...] + b_ref[...]
    logf = jnp.minimum(z, 0.0) - jnp.log1p(jnp.exp(-jnp.abs(z)))
    logf_ref[...] = logf[:, :n_heads]
    cum = logf
    if cumulative:
        k = 1
        while k < seq:
            cum = cum + _shift_rows(cum, k)
            k *= 2
    cum_ref[...] = cum
    cum_t_ref[...] = jnp.transpose(cum)[:cum_t_ref.shape[0], :]


def _logf(z, b_pad, *, n_seq, seq, n_heads, cumulative):
    m, lanes = z.shape
    head_rows = BF16_ROWS_V7X
    return pl.pallas_call(
        functools.partial(_logf_kernel, n_heads=n_heads, seq=seq, cumulative=cumulative), name='logf',
        grid=(n_seq,),
        in_specs=[pl.BlockSpec((seq, lanes), lambda b: (b, 0)), pl.BlockSpec((1, lanes), lambda b: (0, 0))],
        out_specs=[pl.BlockSpec((seq, n_heads), lambda b: (b, 0)),
                   pl.BlockSpec((seq, lanes), lambda b: (b, 0)),
                   pl.BlockSpec((None, head_rows, seq), lambda b: (b, 0, 0))],
        out_shape=[jax.ShapeDtypeStruct((m, n_heads), F32), jax.ShapeDtypeStruct((m, lanes), F32),
                   jax.ShapeDtypeStruct((n_seq, head_rows, seq), F32)],
        compiler_params=_params(("arbitrary",), 16 * _nbytes((seq, lanes), F32)),
    )(z, b_pad)


def _causal_mask(s, row0):
    row = row0 + lax.broadcasted_iota(jnp.int32, s.shape, 0)
    col = lax.broadcasted_iota(jnp.int32, s.shape, 1)
    return jnp.where(col <= row, s, -jnp.inf)


def _nt_dot(a, b):
    return lax.dot_general(a, b, (((1,), (1,)), ((), ())), preferred_element_type=F32)


def _fox_kernel(q_ref, k_ref, v_ref, cc_ref, cr_ref, o_ref, kb_ref, vb_ref, *, tq, nq, scale):
    h = pl.program_id(1)
    qi = pl.program_id(2)

    @pl.when(qi == 0)
    def _cast_kv():
        kb_ref[...] = k_ref[...].astype(BF16)
        vb_ref[...] = v_ref[...].astype(BF16)

    q = q_ref[...]
    lane = lax.broadcasted_iota(jnp.int32, cc_ref.shape, 1)
    cum_q = jnp.sum(jnp.where(lane == h, cc_ref[...], 0.0), axis=1, keepdims=True)
    for n in range(nq):
        @pl.when(qi == n)
        def _block(n=n):
            n_keys = (n + 1) * tq
            s = _nt_dot(q, kb_ref[0:n_keys, :]) * scale
            s = s + cum_q - cr_ref[pl.ds(h, 1), 0:n_keys]
            s = _causal_mask(s, n * tq)
            p = jnp.exp(s - jnp.max(s, axis=1, keepdims=True))
            denom = jnp.sum(p, axis=1, keepdims=True)
            o = jnp.dot(p.astype(BF16), vb_ref[0:n_keys, :], preferred_element_type=F32)
            o_ref[...] = (o / denom).astype(o_ref.dtype)


def _fox_attention(q, k_buf, v_buf, layer, cum_cols, cum_rows, *, n_seq, seq, n_heads, head_dim):
    m = q.shape[0]
    tq = _pick(seq, (512, 256, 128))
    nq = seq // tq
    lanes = cum_cols.shape[1]
    vmem = (4 * _nbytes((seq, head_dim), F32) + 2 * _nbytes((seq, head_dim), BF16)
            + 6 * _nbytes((tq, seq), F32) + 4 * _nbytes((tq, lanes), F32))
    kv_spec = pl.BlockSpec((None, None, None, seq, head_dim), lambda b, h, i: (layer, b, h, 0, 0))
    return pl.pallas_call(
        functools.partial(_fox_kernel, tq=tq, nq=nq, scale=head_dim ** -0.5), name='fox_attn',
        grid=(n_seq, n_heads, nq),
        in_specs=[pl.BlockSpec((tq, head_dim), lambda b, h, i: (b * nq + i, h)), kv_spec, kv_spec,
                  pl.BlockSpec((tq, lanes), lambda b, h, i: (b * nq + i, 0)),
                  pl.BlockSpec((None, cum_rows.shape[1], seq), lambda b, h, i: (b, 0, 0))],
        out_specs=pl.BlockSpec((tq, head_dim), lambda b, h, i: (b * nq + i, h)),
        out_shape=jax.ShapeDtypeStruct((m, n_heads * head_dim), BF16),
        scratch_shapes=[pltpu.VMEM((seq, head_dim), BF16), pltpu.VMEM((seq, head_dim), BF16)],
        compiler_params=_params(("arbitrary", "arbitrary", "arbitrary"), vmem),
    )(q, k_buf, v_buf, cum_cols, cum_rows)


def _diff_lambda(lq, lam_init):
    a = jnp.sum(lq[0:1, :] * lq[1:2, :], axis=1, keepdims=True)
    b = jnp.sum(lq[2:3, :] * lq[3:4, :], axis=1, keepdims=True)
    return jnp.exp(a) - jnp.exp(b) + lam_init


def _sub_norm(o, g, lam_init):
    return o * lax.rsqrt(jnp.mean(o * o, axis=-1, keepdims=True) + EPS) * g * (1.0 - lam_init)


def _diff_kernel(q_ref, k_ref, v_ref, lq_ref, g_ref, o_ref, kb_ref, vb_ref, *, tq, nq, scale, head_dim, lam_init):
    qi = pl.program_id(2)

    @pl.when(qi == 0)
    def _cast_kv():
        kb_ref[...] = k_ref[...].astype(BF16)
        vb_ref[...] = v_ref[...].astype(BF16)

    q = q_ref[...]
    lam = _diff_lambda(lq_ref[...], lam_init)
    for n in range(nq):
        @pl.when(qi == n)
        def _block(n=n):
            n_keys = (n + 1) * tq
            probs = []
            for c in range(2):
                cols = slice(c * head_dim, (c + 1) * head_dim)
                s = _causal_mask(_nt_dot(q[:, cols], kb_ref[c, 0:n_keys, :]) * scale, n * tq)
                p = jnp.exp(s - jnp.max(s, axis=1, keepdims=True))
                probs.append(p / jnp.sum(p, axis=1, keepdims=True))
            a = probs[0] - lam * probs[1]
            o = jnp.dot(a.astype(BF16), vb_ref[0:n_keys, :], preferred_element_type=F32)
            o_ref[...] = _sub_norm(o, g_ref[...], lam_init).astype(o_ref.dtype)


def _diff_attention(q, k_buf, v_buf, lambda_qk, subln_g, layer, *, n_seq, seq, n_heads, head_dim, lam_init):
    m = q.shape[0]
    tq = _pick(seq, (512, 256, 128))
    nq = seq // tq
    width = 2 * head_dim
    vmem = (4 * _nbytes((seq, width), F32) + 2 * _nbytes((seq, width), BF16) + 8 * _nbytes((tq, seq), F32))
    return pl.pallas_call(
        functools.partial(_diff_kernel, tq=tq, nq=nq, scale=head_dim ** -0.5, head_dim=head_dim,
                          lam_init=lam_init), name='diff_attn',
        grid=(n_seq, n_heads, nq),
        in_specs=[pl.BlockSpec((tq, width), lambda b, h, i: (b * nq + i, h)),
                  pl.BlockSpec((None, None, 2, seq, head_dim), lambda b, h, i: (layer, b, h, 0, 0)),
                  pl.BlockSpec((None, None, None, seq, width), lambda b, h, i: (layer, b, h, 0, 0)),
                  pl.BlockSpec((None, 4, head_dim), lambda b, h, i: (layer, 0, 0)),
                  pl.BlockSpec((None, 1, width), lambda b, h, i: (layer, 0, 0))],
        out_specs=pl.BlockSpec((tq, width), lambda b, h, i: (b * nq + i, h)),
        out_shape=jax.ShapeDtypeStruct((m, n_heads * width), BF16),
        scratch_shapes=[pltpu.VMEM((2, seq, head_dim), BF16), pltpu.VMEM((seq, width), BF16)],
        compiler_params=_params(("arbitrary", "arbitrary", "arbitrary"), vmem),
    )(q, k_buf, v_buf, lambda_qk, subln_g.reshape(subln_g.shape[0], 1, width))


def _pool_kernel(u_ref, w_ref, ps_ref, o_ref, *, seq, group_w):
    row = lax.broadcasted_iota(jnp.int32, (seq, 1), 0)
    for g, window in enumerate(POOL_WINDOWS):
        cols = slice(g * group_w, (g + 1) * group_w)
        u = u_ref[:, cols]
        total = u
        k = 1
        while k < window:
            total = total + _shift_rows(total, k)
            k *= 2
        count = jnp.minimum(row + 1, window).astype(F32)
        pooled = total / count - u
        y = jnp.dot(pooled.astype(BF16), w_ref[g].astype(BF16), preferred_element_type=F32)
        o_ref[:, cols] = (y * ps_ref[:, cols]).astype(o_ref.dtype)


def _pool(u, w_pool, pool_scale, layer):
    n_seq, seq, width = u.shape
    group_w = width // len(POOL_WINDOWS)
    return pl.pallas_call(
        functools.partial(_pool_kernel, seq=seq, group_w=group_w), name='pool',
        grid=(n_seq,),
        in_specs=[pl.BlockSpec((None, seq, width), lambda b: (b, 0, 0)),
                  pl.BlockSpec((None,) + w_pool.shape[1:], lambda b: (layer, 0, 0, 0)),
                  pl.BlockSpec((None, 1, width), lambda b: (layer, 0, 0))],
        out_specs=pl.BlockSpec((None, seq, width), lambda b: (b, 0, 0)),
        out_shape=jax.ShapeDtypeStruct((n_seq, seq, width), BF16),
        compiler_params=_params(("arbitrary",), 5 * _nbytes((seq, width), F32)),
    )(u, w_pool, pool_scale.reshape(pool_scale.shape[0], 1, width))


def _split3(x):
    hi = x.astype(BF16)
    r = x - hi.astype(F32)
    mid = r.astype(BF16)
    lo = (r - mid.astype(F32)).astype(BF16)
    return hi, mid, lo


def _row_select(stack_ref, n, rows):
    row = lax.broadcasted_iota(jnp.int32, stack_ref.shape[1:], 0)
    out = jnp.zeros(stack_ref.shape[1:], F32)
    for h in range(n):
        out = jnp.where(row == h, stack_ref[h], out)
    return out


def _decode_kernel(pt_ref, qf_ref, kfn_ref, vfn_ref, lfn_ref, qd_ref, kdn_ref, vdn_ref, lq_ref, g_ref, *rest,
                   n_slots, n_fox, n_diff, head_dim, scale, lam_init):
    del pt_ref
    pages = [rest[5 * k:5 * k + 5] for k in range(n_slots)]
    (of_ref, od_ref, qbf_ref, qbd_ref, mf_ref, lf_ref, accf_ref, cf_ref, md_ref, ld_ref,
     accd_ref) = rest[5 * n_slots:]
    p = pl.program_id(1)
    rows = qf_ref.shape[0]
    row_id = lax.broadcasted_iota(jnp.int32, (rows, head_dim), 0)

    @pl.when(p == 0)
    def _init():
        qf = qf_ref[...]
        qd = qd_ref[...]
        for h in range(n_fox):
            qbf_ref[h] = jnp.where(row_id == h, qf, 0.0).astype(BF16)
            accf_ref[h] = jnp.broadcast_to(vfn_ref[h:h + 1, :], (rows, head_dim))
        for r in range(2 * n_diff):
            qbd_ref[r] = jnp.where(row_id == r, qd, 0.0).astype(BF16)
        for h in range(n_diff):
            accd_ref[h] = jnp.broadcast_to(vdn_ref[h:h + 1, :], (rows, 2 * head_dim))
        mf_ref[...] = jnp.sum(qf.astype(BF16).astype(F32) * kfn_ref[...].astype(BF16).astype(F32),
                              axis=1, keepdims=True) * scale
        md_ref[...] = jnp.sum(qd.astype(BF16).astype(F32) * kdn_ref[...].astype(BF16).astype(F32),
                              axis=1, keepdims=True) * scale
        lf_ref[...] = jnp.ones_like(lf_ref)
        ld_ref[...] = jnp.ones_like(ld_ref)
        cf_ref[...] = lfn_ref[...]

    page = pages[0][2].shape[1]
    later = (lax.broadcasted_iota(jnp.int32, (page, page), 0)
             > lax.broadcasted_iota(jnp.int32, (page, page), 1)).astype(BF16)

    def softmax_step(s, m_ref, l_ref):
        m_new = jnp.maximum(m_ref[...], jnp.max(s, axis=1, keepdims=True))
        alpha = jnp.exp(m_ref[...] - m_new)
        prob = jnp.exp(s - m_new)
        l_ref[...] = alpha * l_ref[...] + jnp.sum(prob, axis=1, keepdims=True)
        m_ref[...] = m_new
        return alpha, prob.astype(BF16)

    def scores(q_stack_ref, k_ref, n):
        s = _nt_dot(q_stack_ref[0], k_ref[0].astype(BF16))
        for r in range(1, n):
            s = s + _nt_dot(q_stack_ref[r], k_ref[r].astype(BF16))
        return s

    def weighted_values(prob, v_refs, h):
        return sum(jnp.dot(prob[:, k * page:(k + 1) * page], v_ref[h].astype(BF16), preferred_element_type=F32)
                   for k, v_ref in enumerate(v_refs))

    decay = cf_ref[...]
    s_fox, s_diff = [], []
    for ck_ref, _, clf_ref, cdk_ref, _ in pages:
        logf_t = clf_ref[...]
        suffix = sum(jnp.dot(t, later, preferred_element_type=F32) for t in _split3(logf_t))
        s_fox.append(scores(qbf_ref, ck_ref, n_fox) * scale + (decay + suffix))
        s_diff.append(scores(qbd_ref, cdk_ref, 2 * n_diff) * scale)
        decay = decay + jnp.sum(logf_t, axis=1, keepdims=True)
    cf_ref[...] = decay

    alpha, prob = softmax_step(jnp.concatenate(s_fox, axis=1), mf_ref, lf_ref)
    for h in range(n_fox):
        accf_ref[h] = alpha * accf_ref[h] + weighted_values(prob, [pg[1] for pg in pages], h)
    alpha, prob = softmax_step(jnp.concatenate(s_diff, axis=1), md_ref, ld_ref)
    for h in range(n_diff):
        accd_ref[h] = alpha * accd_ref[h] + weighted_values(prob, [pg[4] for pg in pages], h)

    @pl.when(p == pl.num_programs(1) - 1)
    def _finish():
        of_ref[...] = _row_select(accf_ref, n_fox, rows) / lf_ref[...]
        lam = _diff_lambda(lq_ref[...], lam_init)
        od_ref[...] = jnp.zeros_like(od_ref)
        for h in range(n_diff):
            a0 = accd_ref[h][2 * h:2 * h + 1, :] / ld_ref[2 * h:2 * h + 1, :]
            a1 = accd_ref[h][2 * h + 1:2 * h + 2, :] / ld_ref[2 * h + 1:2 * h + 2, :]
            od_ref[h:h + 1, :] = _sub_norm(a0 - lam * a1, g_ref[...], lam_init)


def _decode_attention(page_table, qf, kfn, vfn, lfn, qd, kdn, vdn, lambda_qk, subln_g,
                      cache_fox_k, cache_fox_v, logf_t, cache_diff_k, cache_diff_v, layer,
                      *, n_fox, n_diff, head_dim, lam_init):
    n_seq, n_pages = page_table.shape
    rows = qf.shape[1]
    page = cache_fox_k.shape[3]
    width = 2 * head_dim
    vrows = vdn.shape[1]
    n_slots = _pick(n_pages, (4, 2, 1))

    def per_seq(shape):
        return pl.BlockSpec((None,) + shape, lambda b, p, pt: (b, 0, 0))

    def paged(shape, slot):
        return pl.BlockSpec(
            (None, None) + shape,
            lambda b, p, pt: (layer, pt[b, n_pages - 1 - (p * n_slots + slot)]) + (0,) * len(shape))

    in_specs = [per_seq((rows, head_dim)), per_seq((rows, head_dim)), per_seq((rows, head_dim)),
                per_seq((rows, 1)),
                per_seq((rows, head_dim)), per_seq((rows, head_dim)), per_seq((vrows, width)),
                pl.BlockSpec((None, 4, head_dim), lambda b, p, pt: (layer, 0, 0)),
                pl.BlockSpec((None, 1, width), lambda b, p, pt: (layer, 0, 0))]
    caches = []
    for slot in range(n_slots):
        in_specs += [paged((n_fox, page, head_dim), slot), paged((n_fox, page, head_dim), slot),
                     paged((rows, page), slot),
                     paged((2 * n_diff, page, head_dim), slot), paged((n_diff, page, width), slot)]
        caches += [cache_fox_k, cache_fox_v, logf_t, cache_diff_k, cache_diff_v]
    scratch = [pltpu.VMEM((n_fox, rows, head_dim), BF16), pltpu.VMEM((2 * n_diff, rows, head_dim), BF16),
               pltpu.VMEM((rows, 1), F32), pltpu.VMEM((rows, 1), F32),
               pltpu.VMEM((n_fox, rows, head_dim), F32), pltpu.VMEM((rows, 1), F32),
               pltpu.VMEM((rows, 1), F32), pltpu.VMEM((rows, 1), F32),
               pltpu.VMEM((n_diff, rows, width), F32)]
    vmem = 2 * n_slots * (4 * _nbytes((n_fox, page, head_dim), F32) + _nbytes((rows, page), F32)) + (4 << 20)
    grid_spec = pltpu.PrefetchScalarGridSpec(
        num_scalar_prefetch=1, grid=(n_seq, n_pages // n_slots), in_specs=in_specs,
        out_specs=[pl.BlockSpec((None, rows, head_dim), lambda b, p, pt: (b, 0, 0)),
                   pl.BlockSpec((None, vrows, width), lambda b, p, pt: (b, 0, 0))],
        scratch_shapes=scratch)
    return pl.pallas_call(
        functools.partial(_decode_kernel, n_slots=n_slots, n_fox=n_fox, n_diff=n_diff, head_dim=head_dim,
                          scale=head_dim ** -0.5, lam_init=lam_init),
        grid_spec=grid_spec, name='decode_attn',
        out_shape=[jax.ShapeDtypeStruct((n_seq, rows, head_dim), F32),
                   jax.ShapeDtypeStruct((n_seq, vrows, width), F32)],
        compiler_params=_params(("arbitrary", "arbitrary"), vmem),
    )(page_table, qf, kfn, vfn, lfn, qd, kdn, vdn, lambda_qk,
      subln_g.reshape(subln_g.shape[0], 1, width), *caches)


def _rope_tables(pos, head_dim):
    half = head_dim // 2
    inv_freq = ROPE_THETA ** (-jnp.arange(half, dtype=F32) / half)
    ang = pos.astype(F32)[:, None] * inv_freq[None, :]
    cos, sin = jnp.cos(ang), jnp.sin(ang)
    return jnp.concatenate([cos, cos], axis=-1), jnp.concatenate([-sin, sin], axis=-1)


class _Stream:
    def __init__(self, n_rows, n_groups, seq):
        self.m = n_rows
        self.tm = _pick(n_rows, (1024, 512, 256, 128))
        self.tm_wide_k = _pick(n_rows, (256, 128))
        self.tm_norm = _pick(n_rows, (256, 128))
        self.n_groups = n_groups
        self.seq = seq

    def tiles_per_group(self, tm):
        return max(1, self.m // self.n_groups // tm)


def _layer(st, x, mods, w, layer, rope, dims, attention, kv_bufs):
    d = x.shape[1]
    tm, tpg = st.tm, st.tiles_per_group(st.tm)
    tmn, tpgn = st.tm_norm, st.tiles_per_group(st.tm_norm)
    d_ff = w['w_ffn_out'].shape[2]
    n_fox, n_diff, head_dim, pool_w = dims['n_fox'], dims['n_diff'], dims['head_dim'], dims['pool_w']
    fox_w = n_fox * head_dim

    def swiglu_ffn(x, which, norm_idx):
        h = _norm(x, w['norm_g'][layer, norm_idx], mods[3 * norm_idx + 1], mods[3 * norm_idx],
                  tm=tmn, tiles_per_group=tpgn, out_dtype=BF16)
        tn = _pick(d_ff, (256, 128))
        lead = (layer, which)
        (hid,) = _matmul([h], [(0, w['w_ffn_in'], lead, 0), (0, w['w_ffn_in'], lead, d_ff // tn)], [],
                         _ep_swiglu, [BF16], tm=tm, tn=tn, n_tiles=d_ff // tn, name='ffn_in')
        tm2 = st.tm_wide_k
        tn2 = _pick(d, (512, 256, 128))
        (x_new,) = _matmul([hid], [(0, w['w_ffn_out'], lead, 0)],
                           [_tile_extra(x, tm2, tn2), _mod_extra(mods[3 * norm_idx + 2], tn2, st.tiles_per_group(tm2))],
                           functools.partial(_ep_residual, 0.5), [F32], tm=tm2, tn=tn2, n_tiles=d // tn2,
                           name='ffn_out', single_buffer_weights=True)
        return x_new

    x = swiglu_ffn(x, 0, 0)

    h = _norm(x, w['norm_g'][layer, 1], mods[4], mods[3], tm=tmn, tiles_per_group=tpgn, out_dtype=BF16)
    tn = _pick(fox_w, (512, 256, 128))
    nt = fox_w // tn
    w_in = w['w_in']

    def proj(weight, lead, col0, n_tiles, out_dtype, rope_it=False, buf=None):
        if rope_it:
            cos, sin_signed = rope
            rows = cos.shape[0]
            extras = [(cos, (tm, head_dim), lambda j, i: (i % (rows // tm), 0)),
                      (sin_signed, (tm, head_dim), lambda j, i: (i % (rows // tm), 0))]
            ep = functools.partial(_ep_rope, head_dim)
        else:
            extras, ep = [], _ep_plain
        (out,) = _matmul([h], [(0, weight, lead, col0)], extras, ep, [out_dtype], tm=tm, tn=tn, n_tiles=n_tiles,
                         name='proj_rope' if rope_it else 'proj',
                         head_major=None if buf is None else (buf, layer, tpg))
        return out

    fq = proj(w_in, (layer,), 0, nt, BF16)
    fk_buf, fv_buf, dk_buf, dv_buf = kv_bufs if kv_bufs is not None else (None,) * 4
    fk = proj(w_in, (layer,), nt, nt, F32, buf=fk_buf)
    fv = proj(w_in, (layer,), 2 * nt, nt, F32, buf=fv_buf)
    w_d = w['w_in_diff']
    dq = proj(w_d, (layer,), 0, nt, BF16, rope_it=True)
    dk = proj(w_d, (layer,), nt, nt, F32, rope_it=True, buf=dk_buf)
    dv = proj(w_d, (layer,), 2 * nt, nt, F32, buf=dv_buf)
    u = proj(w_d, (layer,), 3 * nt, pool_w // tn, F32)
    (z,) = _matmul([h], [(0, w['w_in_forget'], (layer,), 0)], [], _ep_plain, [F32],
                   tm=tm, tn=LANES_V7X, n_tiles=1, name='proj_forget')

    y_fox, y_diff, y_pool, logf, u_state = attention(fq, fk, fv, z, dq, dk, dv, u)

    tng = _pick(d, (512, 256, 128))
    ntg = d // tng
    (gates,) = _matmul([h], [(0, w['w_gate'], (layer,), 0)],
                       [(w['b_gate'].reshape(-1, 1, 3 * d), (None, 1, tng), lambda j, i: (layer, 0, j))],
                       _ep_sigmoid_bias, [BF16], tm=tm, tn=tng, n_tiles=3 * ntg, name='gates')
    (merged,) = _matmul([y_fox, y_diff, y_pool],
                        [(0, w['w_br_fox'], (layer,), 0), (1, w['w_br_diff'], (layer,), 0),
                         (2, w['w_br_pool'], (layer,), 0)],
                        [_tile_extra(gates, tm, tng, b * ntg) for b in range(3)],
                        _ep_merge, [BF16], tm=tm, tn=tng, n_tiles=ntg, name='merge')
    (x,) = _matmul([merged], [(0, w['w_o'], (layer,), 0)],
                   [_tile_extra(x, tm, tng), _mod_extra(mods[5], tng, tpg)],
                   functools.partial(_ep_residual, 1.0), [F32], tm=tm, tn=tng, n_tiles=ntg, name='w_o')

    x = swiglu_ffn(x, 1, 2)
    return x, (fk, fv, logf, dk, dv, u_state)


def kernel(x_prompt, x_sample, c_prompt, c_sample, page_table, cache_fox_k, cache_fox_v, cache_fox_logf, cache_diff_k, cache_diff_v, state_pool, norm_g, w_ada, b_ada, w_ffn_in, w_ffn_out, w_in, b_forget, lambda_qk, subln_g, w_pool, pool_scale, w_br_fox, w_br_diff, w_br_pool, w_gate, b_gate, w_o, final_norm_g):
    n_seq, seq, d = x_prompt.shape
    n_dec, dec_seq, _ = x_sample.shape
    assert dec_seq == 1 and n_dec <= SAMPLE_ROWS
    depth = norm_g.shape[0]
    n_fox, head_dim = cache_fox_k.shape[3], cache_fox_k.shape[4]
    n_diff = cache_diff_v.shape[3]
    fox_w = n_fox * head_dim
    pool_w = w_pool.shape[1] * w_pool.shape[2]
    pool_state = state_pool.shape[2]
    n_pages, page = page_table.shape[1], cache_fox_k.shape[2]
    past_len = n_pages * page
    dims = dict(n_fox=n_fox, n_diff=n_diff, head_dim=head_dim, pool_w=pool_w)
    rows = SAMPLE_ROWS
    assert n_fox <= rows and 2 * n_diff <= rows and n_fox <= LANES_V7X

    off_ff = 3 * fox_w
    off_d = off_ff + n_fox
    weights = dict(norm_g=norm_g, w_ffn_in=w_ffn_in, w_ffn_out=w_ffn_out, w_in=w_in, w_gate=w_gate,
                   b_gate=b_gate, w_br_fox=w_br_fox, w_br_diff=w_br_diff, w_br_pool=w_br_pool, w_o=w_o,
                   w_in_diff=w_in[:, :, off_d:],
                   w_in_forget=jnp.pad(w_in[:, :, off_ff:off_d], ((0, 0), (0, 0), (0, LANES_V7X - n_fox))))
    b_forget_pad = jnp.pad(b_forget, ((0, 0), (0, LANES_V7X - n_fox)))

    n_c = n_seq + n_dec
    c_rows = -(-n_c // BF16_ROWS_V7X) * BF16_ROWS_V7X
    c_all = jnp.pad(jnp.concatenate([c_prompt, c_sample], axis=0), ((0, c_rows - n_c), (0, 0)))
    tn_ada = _pick(N_MOD * d, (1024, 512, 256, 128))
    mods_p, mods_s = [], []
    for l in range(depth):
        (mod,) = _matmul([c_all], [(0, w_ada, (l,), 0)],
                         [(b_ada.reshape(depth, 1, N_MOD * d), (None, 1, tn_ada), lambda j, i, l=l: (l, 0, j))],
                         _ep_bias, [F32], tm=c_rows, tn=tn_ada, n_tiles=N_MOD * d // tn_ada, name='ada',
                         prologue=_silu_bf16)
        mod = mod.reshape(c_rows, N_MOD, d)
        mods_p.append([mod[:n_seq, k][:, None, :] for k in range(N_MOD)])
        mods_s.append([jnp.pad(mod[n_seq:n_c, k], ((0, rows - n_dec), (0, 0)))[None] for k in range(N_MOD)])

    st_p = _Stream(n_seq * seq, n_seq, seq)
    st_s = _Stream(rows, 1, 1)
    rope_p = _rope_tables(jnp.arange(seq, dtype=jnp.int32), head_dim)
    rope_s = tuple(jnp.broadcast_to(t, (rows, head_dim))
                   for t in _rope_tables(jnp.full((1,), past_len, jnp.int32), head_dim))
    logf_t = jnp.pad(jnp.swapaxes(cache_fox_logf, 2, 3), ((0, 0), (0, 0), (0, rows - n_fox), (0, 0)))
    head_major = (0, 1, 3, 2, 4)
    cache_views = [jnp.transpose(c, head_major) for c in (cache_fox_k, cache_fox_v, cache_diff_k, cache_diff_v)]
    kv_bufs = [jnp.zeros((depth, n_seq, n_fox, seq, head_dim), F32),
               jnp.zeros((depth, n_seq, n_fox, seq, head_dim), F32),
               jnp.zeros((depth, n_seq, 2 * n_diff, seq, head_dim), F32),
               jnp.zeros((depth, n_seq, n_diff, seq, 2 * head_dim), F32)]

    def attention_prompt(l, fq, fk, fv, z, dq, dk, dv, u):
        logf, cum_cols, cum_rows = _logf(z, b_forget_pad[l:l + 1], n_seq=n_seq, seq=seq, n_heads=n_fox,
                                         cumulative=True)
        y_fox = _fox_attention(fq, fk, fv, l, cum_cols, cum_rows, n_seq=n_seq, seq=seq, n_heads=n_fox,
                               head_dim=head_dim)
        lam_init = 0.8 - 0.6 * math.exp(-0.3 * l)
        y_diff = _diff_attention(dq, dk, dv, lambda_qk, subln_g, l, n_seq=n_seq, seq=seq, n_heads=n_diff,
                                 head_dim=head_dim, lam_init=lam_init)
        u3 = u.reshape(n_seq, seq, pool_w)
        y_pool = _pool(u3, w_pool, pool_scale, l).reshape(n_seq * seq, pool_w)
        return y_fox, y_diff, y_pool, logf, u3[:, seq - pool_state:]

    def attention_sample(l, fq, fk, fv, z, dq, dk, dv, u):
        logf, cum_cols, _ = _logf(z, b_forget_pad[l:l + 1], n_seq=1, seq=rows, n_heads=n_fox, cumulative=False)

        def heads(a, n, width):
            a = a[:n_dec].astype(F32).reshape(n_dec, n, width)
            return jnp.pad(a, ((0, 0), (0, -(-n // SUBLANES_V7X) * SUBLANES_V7X - n), (0, 0)))

        def heads16(a, n):
            a = heads(a, n, head_dim)
            return jnp.pad(a, ((0, 0), (0, rows - a.shape[1]), (0, 0)))

        lfn = jnp.pad(cum_cols[:n_dec, :n_fox], ((0, 0), (0, rows - n_fox)))[:, :, None]
        lam_init = 0.8 - 0.6 * math.exp(-0.3 * l)
        o_fox, o_diff = _decode_attention(
            page_table, heads16(fq, n_fox), heads16(fk, n_fox), heads16(fv, n_fox), lfn,
            heads16(dq, 2 * n_diff), heads16(dk, 2 * n_diff), heads(dv, n_diff, 2 * head_dim),
            lambda_qk, subln_g, cache_views[0], cache_views[1], logf_t, cache_views[2], cache_views[3], l,
            n_fox=n_fox, n_diff=n_diff, head_dim=head_dim, lam_init=lam_init)

        def unheads(o, n):
            o = o[:, :n].reshape(n_dec, -1)
            return jnp.pad(o, ((0, rows - n_dec), (0, 0))).astype(BF16)

        u_ext = jnp.concatenate([state_pool[l], u[:n_dec, None, :]], axis=1)
        y_pool = _pool(u_ext, w_pool, pool_scale, l)[:, -1]
        y_pool = jnp.pad(y_pool, ((0, rows - n_dec), (0, 0)))
        return unheads(o_fox, n_fox), unheads(o_diff, n_diff), y_pool, logf, u_ext[:, 1:]

    xp = x_prompt.reshape(n_seq * seq, d)
    xs = jnp.pad(x_sample.reshape(n_dec, d), ((0, rows - n_dec), (0, 0)))
    states_p, states_s = [], []
    for l in range(depth):
        xp, sp = _layer(st_p, xp, mods_p[l], weights, l, rope_p, dims, functools.partial(attention_prompt, l),
                        kv_bufs)
        kv_bufs = [sp[0], sp[1], sp[3], sp[4]]
        xs, ss = _layer(st_s, xs, mods_s[l], weights, l, rope_s, dims, functools.partial(attention_sample, l),
                        None)
        states_p.append(sp)
        states_s.append(ss)

    y_prompt = _norm(xp, final_norm_g, None, None, tm=st_p.tm_norm, tiles_per_group=1, out_dtype=F32)
    y_sample = _norm(xs, final_norm_g, None, None, tm=st_s.tm_norm, tiles_per_group=1, out_dtype=F32)

    def stack_p(i, shape):
        return jnp.stack([s[i].reshape(shape) for s in states_p], axis=0)

    def stack_s(i, shape):
        return jnp.stack([s[i][:n_dec].reshape(shape) for s in states_s], axis=0)

    return (y_prompt.reshape(n_seq, seq, d), y_sample[:n_dec].reshape(n_dec, 1, d),
            jnp.transpose(kv_bufs[0], head_major), jnp.transpose(kv_bufs[1], head_major),
            stack_p(2, (n_seq, seq, n_fox)),
            jnp.transpose(kv_bufs[2], head_major), jnp.transpose(kv_bufs[3], head_major),
            jnp.stack([s[5] for s in states_p], axis=0),
            stack_s(0, (n_dec, 1, n_fox, head_dim)), stack_s(1, (n_dec, 1, n_fox, head_dim)),
            stack_s(2, (n_dec, 1, n_fox)),
            stack_s(3, (n_dec, 1, 2 * n_diff, head_dim)), stack_s(4, (n_dec, 1, n_diff, 2 * head_dim)),
            jnp.stack([s[5] for s in states_s], axis=0))
```

```python
import functools
import math

import jax
import jax.numpy as jnp
from jax import lax
from jax.experimental import pallas as pl
from jax.experimental.pallas import tpu as pltpu

F32 = jnp.float32
BF16 = jnp.bfloat16

LANES_V7X = 128
SUBLANES_V7X = 8
BF16_ROWS_V7X = 16
VMEM_CAP_V7X = 58 * 1024 * 1024

POOL_WINDOWS = (2, 4, 8, 16)
ROPE_THETA = 10000.0
EPS = 1e-6
N_MOD = 9
SAMPLE_ROWS = BF16_ROWS_V7X


def _pick(n, prefs):
    for p in prefs:
        if n % p == 0:
            return p
    return n


def _nbytes(shape, dtype):
    return math.prod(shape) * jnp.dtype(dtype).itemsize


def _params(sem, vmem_estimate):
    limit = min(VMEM_CAP_V7X, int(vmem_estimate * 5 // 4) + (4 << 20))
    return pltpu.CompilerParams(dimension_semantics=sem, vmem_limit_bytes=limit)


def _w_imap(lead, col0, j, i):
    return (*lead, 0, col0 + j)


def _x_imap(j, i):
    return (i, 0)


def _out_imap(j, i):
    return (i, j)


def _mm_kernel(*refs, n_x, n_p, n_e, n_o, n_alias, x_of, prologue, epilogue, head_w):
    x_refs = refs[:n_x]
    w_refs = refs[n_x:n_x + n_p]
    e_refs = refs[n_x + n_p:n_x + n_p + n_e]
    n_in = n_x + n_p + n_e + n_alias
    o_refs = refs[n_in:n_in + n_o]
    wb_refs = refs[n_in + n_o:]

    @pl.when(pl.program_id(1) == 0)
    def _cast_weights():
        for w_ref, wb_ref in zip(w_refs, wb_refs):
            wb_ref[...] = w_ref[...].astype(BF16)

    xs = [x_ref[...] for x_ref in x_refs]
    if prologue is not None:
        xs = [prologue(x) for x in xs]
    accs = [jnp.dot(xs[x_of[p]], wb_refs[p][...], preferred_element_type=F32) for p in range(n_p)]
    outs = epilogue(accs, [e_ref[...] for e_ref in e_refs])
    for o_ref, o in zip(o_refs, outs):
        if head_w is None:
            o_ref[...] = o.astype(o_ref.dtype)
        else:
            for h in range(o_ref.shape[0]):
                o_ref[h] = o[:, h * head_w:(h + 1) * head_w].astype(o_ref.dtype)


def _matmul(xs, products, extras, epilogue, out_dtypes, *, tm, tn, n_tiles, name, prologue=None, head_major=None,
            single_buffer_weights=False):
    m = xs[0].shape[0]
    grid = (n_tiles, m // tm)
    in_specs, vmem = [], 0
    for x in xs:
        in_specs.append(pl.BlockSpec((tm, x.shape[1]), _x_imap))
        vmem += 2 * _nbytes((tm, x.shape[1]), x.dtype)
    scratch = []
    for _, w, lead, col0 in products:
        k = w.shape[-2]
        mode = dict(pipeline_mode=pl.Buffered(1)) if single_buffer_weights else {}
        in_specs.append(pl.BlockSpec((None,) * len(lead) + (k, tn), functools.partial(_w_imap, lead, col0),
                                     **mode))
        scratch.append(pltpu.VMEM((k, tn), BF16))
        vmem += (1 if single_buffer_weights else 2) * _nbytes((k, tn), w.dtype) + _nbytes((k, tn), BF16)
    for arr, block, imap in extras:
        in_specs.append(pl.BlockSpec(block, imap))
        vmem += 2 * _nbytes([b for b in block if b is not None], arr.dtype)
    operands = [*xs, *[p[1] for p in products], *[e[0] for e in extras]]
    aliases, head_w = {}, None
    if head_major is None:
        out_shape = [jax.ShapeDtypeStruct((m, n_tiles * tn), dt) for dt in out_dtypes]
        out_specs = [pl.BlockSpec((tm, tn), _out_imap) for _ in out_dtypes]
    else:
        buf, layer, seq_tiles = head_major
        head_w = buf.shape[-1]
        out_shape = [jax.ShapeDtypeStruct(buf.shape, buf.dtype)]
        out_specs = [pl.BlockSpec((None, None, tn // head_w, tm, head_w),
                                  lambda j, i: (layer, i // seq_tiles, j, i % seq_tiles, 0))]
        in_specs.append(pl.BlockSpec(memory_space=pl.ANY))
        aliases = {len(operands): 0}
        operands.append(buf)
    vmem += sum(2 * _nbytes((tm, tn), dt) for dt in out_dtypes) + 4 * len(products) * _nbytes((tm, tn), F32)
    body = functools.partial(
        _mm_kernel, n_x=len(xs), n_p=len(products), n_e=len(extras), n_o=len(out_dtypes),
        n_alias=len(aliases), x_of=tuple(p[0] for p in products), prologue=prologue, epilogue=epilogue,
        head_w=head_w)
    outs = pl.pallas_call(
        body, grid=grid, in_specs=in_specs, out_specs=out_specs, out_shape=out_shape,
        scratch_shapes=scratch, name=name, input_output_aliases=aliases,
        compiler_params=_params(("arbitrary", "arbitrary"), vmem),
    )(*operands)
    return outs


def _mod_extra(arr, tn, tiles_per_group):
    _, r, _ = arr.shape
    return (arr, (None, r, tn), lambda j, i: (i // tiles_per_group, 0, j))


def _tile_extra(arr, tm, tn, col0=0):
    return (arr, (tm, tn), lambda j, i: (i, col0 + j))


def _ep_plain(accs, extras):
    return [accs[0]]


def _ep_bias(accs, extras):
    return [accs[0] + extras[0]]


def _ep_swiglu(accs, extras):
    gate, up = accs
    return [gate * jax.nn.sigmoid(gate) * up]


def _ep_sigmoid_bias(accs, extras):
    return [jax.nn.sigmoid(accs[0] + extras[0])]


def _ep_residual(coef, accs, extras):
    x_res, gate = extras
    return [x_res + (coef * gate) * accs[0]]


def _ep_merge(accs, extras):
    out = extras[0].astype(F32) * accs[0]
    for g, a in zip(extras[1:], accs[1:]):
        out = out + g.astype(F32) * a
    return [out]


def _ep_rope(head_dim, accs, extras):
    cos, sin_signed = extras
    acc = accs[0]
    heads = []
    for h in range(acc.shape[1] // head_dim):
        a = acc[:, h * head_dim:(h + 1) * head_dim]
        heads.append(a * cos + pltpu.roll(a, head_dim // 2, 1) * sin_signed)
    return [jnp.concatenate(heads, axis=1) if len(heads) > 1 else heads[0]]


def _silu_bf16(x):
    return (x * jax.nn.sigmoid(x)).astype(BF16)


def _norm_kernel(x_ref, g_ref, *rest, modulated):
    x = x_ref[...]
    y = x * lax.rsqrt(jnp.mean(x * x, axis=-1, keepdims=True) + EPS) * g_ref[...]
    if modulated:
        scale_ref, shift_ref, o_ref = rest
        y = y * (1.0 + scale_ref[...]) + shift_ref[...]
    else:
        (o_ref,) = rest
    o_ref[...] = y.astype(o_ref.dtype)


def _norm(x, g, scale, shift, *, tm, tiles_per_group, out_dtype):
    m, d = x.shape
    in_specs = [pl.BlockSpec((tm, d), lambda i: (i, 0)), pl.BlockSpec((1, d), lambda i: (0, 0))]
    args = [x, g.reshape(1, d)]
    if scale is not None:
        r = scale.shape[1]
        spec = pl.BlockSpec((None, r, d), lambda i: (i // tiles_per_group, 0, 0))
        in_specs += [spec, spec]
        args += [scale, shift]
    vmem = 6 * _nbytes((tm, d), F32)
    return pl.pallas_call(
        functools.partial(_norm_kernel, modulated=scale is not None), name='norm',
        grid=(m // tm,), in_specs=in_specs, out_specs=pl.BlockSpec((tm, d), lambda i: (i, 0)),
        out_shape=jax.ShapeDtypeStruct((m, d), out_dtype),
        compiler_params=_params(("arbitrary",), vmem),
    )(*args)


def _shift_rows(x, k):
    row = lax.broadcasted_iota(jnp.int32, x.shape, 0)
    return jnp.where(row >= k, pltpu.roll(x, k, 0), 0.0)


def _logf_kernel(z_ref, b_ref, logf_ref, cum_ref, cum_t_ref, *, n_heads, seq, cumulative):
    z = z_ref[...] + b_ref[...]
    logf = jnp.minimum(z, 0.0) - jnp.log1p(jnp.exp(-jnp.abs(z)))
    logf_ref[...] = logf[:, :n_heads]
    cum = logf
    if cumulative:
        k = 1
        while k < seq:
            cum = cum + _shift_rows(cum, k)
            k *= 2
    cum_ref[...] = cum
    cum_t_ref[...] = jnp.transpose(cum)[:cum_t_ref.shape[0], :]


def _logf(z, b_pad, *, n_seq, seq, n_heads, cumulative):
    m, lanes = z.shape
    head_rows = BF16_ROWS_V7X
    return pl.pallas_call(
        functools.partial(_logf_kernel, n_heads=n_heads, seq=seq, cumulative=cumulative), name='logf',
        grid=(n_seq,),
        in_specs=[pl.BlockSpec((seq, lanes), lambda b: (b, 0)), pl.BlockSpec((1, lanes), lambda b: (0, 0))],
        out_specs=[pl.BlockSpec((seq, n_heads), lambda b: (b, 0)),
                   pl.BlockSpec((seq, lanes), lambda b: (b, 0)),
                   pl.BlockSpec((None, head_rows, seq), lambda b: (b, 0, 0))],
        out_shape=[jax.ShapeDtypeStruct((m, n_heads), F32), jax.ShapeDtypeStruct((m, lanes), F32),
                   jax.ShapeDtypeStruct((n_seq, head_rows, seq), F32)],
        compiler_params=_params(("arbitrary",), 16 * _nbytes((seq, lanes), F32)),
    )(z, b_pad)


def _causal_mask(s):
    row = lax.broadcasted_iota(jnp.int32, s.shape, 0)
    col = lax.broadcasted_iota(jnp.int32, s.shape, 1)
    return jnp.where(col <= row, s, -jnp.inf)


def _softmax_accumulate(s, v, state):
    m_chunk = jnp.max(s, axis=1, keepdims=True)
    if state is None:
        p = jnp.exp(s - m_chunk)
        return m_chunk, jnp.sum(p, axis=1, keepdims=True), jnp.dot(p.astype(BF16), v, preferred_element_type=F32)
    m, denom, acc = state
    m_new = jnp.maximum(m, m_chunk)
    alpha = jnp.exp(m - m_new)
    p = jnp.exp(s - m_new)
    return (m_new, alpha * denom + jnp.sum(p, axis=1, keepdims=True),
            alpha * acc + jnp.dot(p.astype(BF16), v, preferred_element_type=F32))


def _nt_dot(a, b):
    return lax.dot_general(a, b, (((1,), (1,)), ((), ())), preferred_element_type=F32)


def _fox_kernel(q_ref, k_ref, v_ref, cc_ref, cr_ref, o_ref, kb_ref, vb_ref, *, tq, nq, scale):
    h = pl.program_id(1)
    qi = pl.program_id(2)

    @pl.when(qi == 0)
    def _cast_kv():
        kb_ref[...] = k_ref[...].astype(BF16)
        vb_ref[...] = v_ref[...].astype(BF16)

    q = q_ref[...]
    lane = lax.broadcasted_iota(jnp.int32, cc_ref.shape, 1)
    cum_q = jnp.sum(jnp.where(lane == h, cc_ref[...], 0.0), axis=1, keepdims=True)
    for n in range(nq):
        @pl.when(qi == n)
        def _block(n=n):
            state = None
            for c in range(n + 1):
                keys = slice(c * tq, (c + 1) * tq)
                s = _nt_dot(q, kb_ref[keys, :]) * scale + (cum_q - cr_ref[pl.ds(h, 1), keys])
                if c == n:
                    s = _causal_mask(s)
                state = _softmax_accumulate(s, vb_ref[keys, :], state)
            _, denom, acc = state
            o_ref[...] = (acc / denom).astype(o_ref.dtype)


def _fox_attention(q, k_buf, v_buf, layer, cum_cols, cum_rows, *, n_seq, seq, n_heads, head_dim):
    m = q.shape[0]
    tq = _pick(seq, (512, 256, 128))
    nq = seq // tq
    lanes = cum_cols.shape[1]
    vmem = (4 * _nbytes((seq, head_dim), F32) + 2 * _nbytes((seq, head_dim), BF16)
            + 6 * _nbytes((tq, seq), F32) + 4 * _nbytes((tq, lanes), F32))
    kv_spec = pl.BlockSpec((None, None, None, seq, head_dim), lambda b, h, i: (layer, b, h, 0, 0))
    return pl.pallas_call(
        functools.partial(_fox_kernel, tq=tq, nq=nq, scale=head_dim ** -0.5), name='fox_attn',
        grid=(n_seq, n_heads, nq),
        in_specs=[pl.BlockSpec((tq, head_dim), lambda b, h, i: (b * nq + i, h)), kv_spec, kv_spec,
                  pl.BlockSpec((tq, lanes), lambda b, h, i: (b * nq + i, 0)),
                  pl.BlockSpec((None, cum_rows.shape[1], seq), lambda b, h, i: (b, 0, 0))],
        out_specs=pl.BlockSpec((tq, head_dim), lambda b, h, i: (b * nq + i, h)),
        out_shape=jax.ShapeDtypeStruct((m, n_heads * head_dim), BF16),
        scratch_shapes=[pltpu.VMEM((seq, head_dim), BF16), pltpu.VMEM((seq, head_dim), BF16)],
        compiler_params=_params(("arbitrary", "arbitrary", "arbitrary"), vmem),
    )(q, k_buf, v_buf, cum_cols, cum_rows)


def _diff_lambda(lq, lam_init):
    a = jnp.sum(lq[0:1, :] * lq[1:2, :], axis=1, keepdims=True)
    b = jnp.sum(lq[2:3, :] * lq[3:4, :], axis=1, keepdims=True)
    return jnp.exp(a) - jnp.exp(b) + lam_init


def _sub_norm(o, g, lam_init):
    return o * lax.rsqrt(jnp.mean(o * o, axis=-1, keepdims=True) + EPS) * g * (1.0 - lam_init)


def _diff_kernel(q_ref, k_ref, v_ref, lq_ref, g_ref, o_ref, kb_ref, vb_ref, *, tq, nq, scale, head_dim, lam_init):
    qi = pl.program_id(2)

    @pl.when(qi == 0)
    def _cast_kv():
        kb_ref[...] = k_ref[...].astype(BF16)
        vb_ref[...] = v_ref[...].astype(BF16)

    q = q_ref[...]
    lam = _diff_lambda(lq_ref[...], lam_init)
    for n in range(nq):
        @pl.when(qi == n)
        def _block(n=n):
            states = [None, None]
            for c in range(n + 1):
                keys = slice(c * tq, (c + 1) * tq)
                for comp in range(2):
                    cols = slice(comp * head_dim, (comp + 1) * head_dim)
                    s = _nt_dot(q[:, cols], kb_ref[comp, keys, :]) * scale
                    if c == n:
                        s = _causal_mask(s)
                    states[comp] = _softmax_accumulate(s, vb_ref[keys, :], states[comp])
            (_, d0, acc0), (_, d1, acc1) = states
            o = acc0 / d0 - lam * (acc1 / d1)
            o_ref[...] = _sub_norm(o, g_ref[...], lam_init).astype(o_ref.dtype)


def _diff_attention(q, k_buf, v_buf, lambda_qk, subln_g, layer, *, n_seq, seq, n_heads, head_dim, lam_init):
    m = q.shape[0]
    tq = _pick(seq, (512, 256, 128))
    nq = seq // tq
    width = 2 * head_dim
    vmem = (4 * _nbytes((seq, width), F32) + 2 * _nbytes((seq, width), BF16) + 8 * _nbytes((tq, seq), F32))
    return pl.pallas_call(
        functools.partial(_diff_kernel, tq=tq, nq=nq, scale=head_dim ** -0.5, head_dim=head_dim,
                          lam_init=lam_init), name='diff_attn',
        grid=(n_seq, n_heads, nq),
        in_specs=[pl.BlockSpec((tq, width), lambda b, h, i: (b * nq + i, h)),
                  pl.BlockSpec((None, None, 2, seq, head_dim), lambda b, h, i: (layer, b, h, 0, 0)),
                  pl.BlockSpec((None, None, None, seq, width), lambda b, h, i: (layer, b, h, 0, 0)),
                  pl.BlockSpec((None, 4, head_dim), lambda b, h, i: (layer, 0, 0)),
                  pl.BlockSpec((None, 1, width), lambda b, h, i: (layer, 0, 0))],
        out_specs=pl.BlockSpec((tq, width), lambda b, h, i: (b * nq + i, h)),
        out_shape=jax.ShapeDtypeStruct((m, n_heads * width), BF16),
        scratch_shapes=[pltpu.VMEM((2, seq, head_dim), BF16), pltpu.VMEM((seq, width), BF16)],
        compiler_params=_params(("arbitrary", "arbitrary", "arbitrary"), vmem),
    )(q, k_buf, v_buf, lambda_qk, subln_g.reshape(subln_g.shape[0], 1, width))


def _pool_kernel(u_ref, w_ref, ps_ref, o_ref, *, seq, group_w):
    row = lax.broadcasted_iota(jnp.int32, (seq, 1), 0)
    for g, window in enumerate(POOL_WINDOWS):
        cols = slice(g * group_w, (g + 1) * group_w)
        u = u_ref[:, cols]
        total = u
        k = 1
        while k < window:
            total = total + _shift_rows(total, k)
            k *= 2
        count = jnp.minimum(row + 1, window).astype(F32)
        pooled = total / count - u
        y = jnp.dot(pooled.astype(BF16), w_ref[g].astype(BF16), preferred_element_type=F32)
        o_ref[:, cols] = (y * ps_ref[:, cols]).astype(o_ref.dtype)


def _pool(u, w_pool, pool_scale, layer):
    n_seq, seq, width = u.shape
    group_w = width // len(POOL_WINDOWS)
    return pl.pallas_call(
        functools.partial(_pool_kernel, seq=seq, group_w=group_w), name='pool',
        grid=(n_seq,),
        in_specs=[pl.BlockSpec((None, seq, width), lambda b: (b, 0, 0)),
                  pl.BlockSpec((None,) + w_pool.shape[1:], lambda b: (layer, 0, 0, 0)),
                  pl.BlockSpec((None, 1, width), lambda b: (layer, 0, 0))],
        out_specs=pl.BlockSpec((None, seq, width), lambda b: (b, 0, 0)),
        out_shape=jax.ShapeDtypeStruct((n_seq, seq, width), BF16),
        compiler_params=_params(("arbitrary",), 5 * _nbytes((seq, width), F32)),
    )(u, w_pool, pool_scale.reshape(pool_scale.shape[0], 1, width))


def _split3(x):
    hi = x.astype(BF16)
    r = x - hi.astype(F32)
    mid = r.astype(BF16)
    lo = (r - mid.astype(F32)).astype(BF16)
    return hi, mid, lo


def _row_select(stack_ref, n, rows):
    row = lax.broadcasted_iota(jnp.int32, stack_ref.shape[1:], 0)
    out = jnp.zeros(stack_ref.shape[1:], F32)
    for h in range(n):
        out = jnp.where(row == h, stack_ref[h], out)
    return out


def _decode_kernel(pt_ref, qf_ref, kfn_ref, vfn_ref, lfn_ref, qd_ref, kdn_ref, vdn_ref, lq_ref, g_ref, *rest,
                   n_slots, n_fox, n_diff, head_dim, scale, lam_init):
    del pt_ref
    pages = [rest[5 * k:5 * k + 5] for k in range(n_slots)]
    (of_ref, od_ref, qbf_ref, qbd_ref, mf_ref, lf_ref, accf_ref, cf_ref, md_ref, ld_ref,
     accd_ref) = rest[5 * n_slots:]
    p = pl.program_id(1)
    rows = qf_ref.shape[0]
    row_id = lax.broadcasted_iota(jnp.int32, (rows, head_dim), 0)

    @pl.when(p == 0)
    def _init():
        qf = qf_ref[...]
        qd = qd_ref[...]
        for h in range(n_fox):
            qbf_ref[h] = jnp.where(row_id == h, qf, 0.0).astype(BF16)
            accf_ref[h] = jnp.broadcast_to(vfn_ref[h:h + 1, :], (rows, head_dim))
        for r in range(2 * n_diff):
            qbd_ref[r] = jnp.where(row_id == r, qd, 0.0).astype(BF16)
        for h in range(n_diff):
            accd_ref[h] = jnp.broadcast_to(vdn_ref[h:h + 1, :], (rows, 2 * head_dim))
        mf_ref[...] = jnp.sum(qf.astype(BF16).astype(F32) * kfn_ref[...].astype(BF16).astype(F32),
                              axis=1, keepdims=True) * scale
        md_ref[...] = jnp.sum(qd.astype(BF16).astype(F32) * kdn_ref[...].astype(BF16).astype(F32),
                              axis=1, keepdims=True) * scale
        lf_ref[...] = jnp.ones_like(lf_ref)
        ld_ref[...] = jnp.ones_like(ld_ref)
        cf_ref[...] = lfn_ref[...]

    page = pages[0][2].shape[1]
    later = (lax.broadcasted_iota(jnp.int32, (page, page), 0)
             > lax.broadcasted_iota(jnp.int32, (page, page), 1)).astype(BF16)

    def softmax_step(s, m_ref, l_ref):
        m_new = jnp.maximum(m_ref[...], jnp.max(s, axis=1, keepdims=True))
        alpha = jnp.exp(m_ref[...] - m_new)
        prob = jnp.exp(s - m_new)
        l_ref[...] = alpha * l_ref[...] + jnp.sum(prob, axis=1, keepdims=True)
        m_ref[...] = m_new
        return alpha, prob.astype(BF16)

    def scores(q_stack_ref, k_ref, n):
        s = _nt_dot(q_stack_ref[0], k_ref[0].astype(BF16))
        for r in range(1, n):
            s = s + _nt_dot(q_stack_ref[r], k_ref[r].astype(BF16))
        return s

    def weighted_values(prob, v_refs, h):
        return sum(jnp.dot(prob[:, k * page:(k + 1) * page], v_ref[h].astype(BF16), preferred_element_type=F32)
                   for k, v_ref in enumerate(v_refs))

    decay = cf_ref[...]
    s_fox, s_diff = [], []
    for ck_ref, _, clf_ref, cdk_ref, _ in pages:
        logf_t = clf_ref[...]
        suffix = sum(jnp.dot(t, later, preferred_element_type=F32) for t in _split3(logf_t))
        s_fox.append(scores(qbf_ref, ck_ref, n_fox) * scale + (decay + suffix))
        s_diff.append(scores(qbd_ref, cdk_ref, 2 * n_diff) * scale)
        decay = decay + jnp.sum(logf_t, axis=1, keepdims=True)
    cf_ref[...] = decay

    alpha, prob = softmax_step(jnp.concatenate(s_fox, axis=1), mf_ref, lf_ref)
    for h in range(n_fox):
        accf_ref[h] = alpha * accf_ref[h] + weighted_values(prob, [pg[1] for pg in pages], h)
    alpha, prob = softmax_step(jnp.concatenate(s_diff, axis=1), md_ref, ld_ref)
    for h in range(n_diff):
        accd_ref[h] = alpha * accd_ref[h] + weighted_values(prob, [pg[4] for pg in pages], h)

    @pl.when(p == pl.num_programs(1) - 1)
    def _finish():
        of_ref[...] = _row_select(accf_ref, n_fox, rows) / lf_ref[...]
        lam = _diff_lambda(lq_ref[...], lam_init)
        od_ref[...] = jnp.zeros_like(od_ref)
        for h in range(n_diff):
            a0 = accd_ref[h][2 * h:2 * h + 1, :] / ld_ref[2 * h:2 * h + 1, :]
            a1 = accd_ref[h][2 * h + 1:2 * h + 2, :] / ld_ref[2 * h + 1:2 * h + 2, :]
            od_ref[h:h + 1, :] = _sub_norm(a0 - lam * a1, g_ref[...], lam_init)


def _decode_attention(page_table, qf, kfn, vfn, lfn, qd, kdn, vdn, lambda_qk, subln_g,
                      cache_fox_k, cache_fox_v, logf_t, cache_diff_k, cache_diff_v, layer,
                      *, n_fox, n_diff, head_dim, lam_init):
    n_seq, n_pages = page_table.shape
    rows = qf.shape[1]
    page = cache_fox_k.shape[3]
    width = 2 * head_dim
    vrows = vdn.shape[1]
    n_slots = _pick(n_pages, (4, 2, 1))

    def per_seq(shape):
        return pl.BlockSpec((None,) + shape, lambda b, p, pt: (b, 0, 0))

    def paged(shape, slot):
        return pl.BlockSpec(
            (None, None) + shape,
            lambda b, p, pt: (layer, pt[b, n_pages - 1 - (p * n_slots + slot)]) + (0,) * len(shape))

    in_specs = [per_seq((rows, head_dim)), per_seq((rows, head_dim)), per_seq((rows, head_dim)),
                per_seq((rows, 1)),
                per_seq((rows, head_dim)), per_seq((rows, head_dim)), per_seq((vrows, width)),
                pl.BlockSpec((None, 4, head_dim), lambda b, p, pt: (layer, 0, 0)),
                pl.BlockSpec((None, 1, width), lambda b, p, pt: (layer, 0, 0))]
    caches = []
    for slot in range(n_slots):
        in_specs += [paged((n_fox, page, head_dim), slot), paged((n_fox, page, head_dim), slot),
                     paged((rows, page), slot),
                     paged((2 * n_diff, page, head_dim), slot), paged((n_diff, page, width), slot)]
        caches += [cache_fox_k, cache_fox_v, logf_t, cache_diff_k, cache_diff_v]
    scratch = [pltpu.VMEM((n_fox, rows, head_dim), BF16), pltpu.VMEM((2 * n_diff, rows, head_dim), BF16),
               pltpu.VMEM((rows, 1), F32), pltpu.VMEM((rows, 1), F32),
               pltpu.VMEM((n_fox, rows, head_dim), F32), pltpu.VMEM((rows, 1), F32),
               pltpu.VMEM((rows, 1), F32), pltpu.VMEM((rows, 1), F32),
               pltpu.VMEM((n_diff, rows, width), F32)]
    vmem = 2 * n_slots * (4 * _nbytes((n_fox, page, head_dim), F32) + _nbytes((rows, page), F32)) + (4 << 20)
    grid_spec = pltpu.PrefetchScalarGridSpec(
        num_scalar_prefetch=1, grid=(n_seq, n_pages // n_slots), in_specs=in_specs,
        out_specs=[pl.BlockSpec((None, rows, head_dim), lambda b, p, pt: (b, 0, 0)),
                   pl.BlockSpec((None, vrows, width), lambda b, p, pt: (b, 0, 0))],
        scratch_shapes=scratch)
    return pl.pallas_call(
        functools.partial(_decode_kernel, n_slots=n_slots, n_fox=n_fox, n_diff=n_diff, head_dim=head_dim,
                          scale=head_dim ** -0.5, lam_init=lam_init),
        grid_spec=grid_spec, name='decode_attn',
        out_shape=[jax.ShapeDtypeStruct((n_seq, rows, head_dim), F32),
                   jax.ShapeDtypeStruct((n_seq, vrows, width), F32)],
        compiler_params=_params(("arbitrary", "arbitrary"), vmem),
    )(page_table, qf, kfn, vfn, lfn, qd, kdn, vdn, lambda_qk,
      subln_g.reshape(subln_g.shape[0], 1, width), *caches)


def _rope_tables(pos, head_dim):
    half = head_dim // 2
    inv_freq = ROPE_THETA ** (-jnp.arange(half, dtype=F32) / half)
    ang = pos.astype(F32)[:, None] * inv_freq[None, :]
    cos, sin = jnp.cos(ang), jnp.sin(ang)
    return jnp.concatenate([cos, cos], axis=-1), jnp.concatenate([-sin, sin], axis=-1)


class _Stream:
    def __init__(self, n_rows, n_groups, seq):
        self.m = n_rows
        self.tm = _pick(n_rows, (1024, 512, 256, 128))
        self.tm_wide_k = _pick(n_rows, (256, 128))
        self.tm_norm = _pick(n_rows, (256, 128))
        self.n_groups = n_groups
        self.seq = seq

    def tiles_per_group(self, tm):
        return max(1, self.m // self.n_groups // tm)


def _layer(st, x, mods, w, layer, rope, dims, attention, kv_bufs):
    d = x.shape[1]
    tm, tpg = st.tm, st.tiles_per_group(st.tm)
    tmn, tpgn = st.tm_norm, st.tiles_per_group(st.tm_norm)
    d_ff = w['w_ffn_out'].shape[2]
    n_fox, n_diff, head_dim, pool_w = dims['n_fox'], dims['n_diff'], dims['head_dim'], dims['pool_w']
    fox_w = n_fox * head_dim

    def swiglu_ffn(x, which, norm_idx):
        h = _norm(x, w['norm_g'][layer, norm_idx], mods[3 * norm_idx + 1], mods[3 * norm_idx],
                  tm=tmn, tiles_per_group=tpgn, out_dtype=BF16)
        tn = _pick(d_ff, (256, 128))
        lead = (layer, which)
        (hid,) = _matmul([h], [(0, w['w_ffn_in'], lead, 0), (0, w['w_ffn_in'], lead, d_ff // tn)], [],
                         _ep_swiglu, [BF16], tm=tm, tn=tn, n_tiles=d_ff // tn, name='ffn_in')
        tm2 = st.tm_wide_k
        tn2 = _pick(d, (512, 256, 128))
        (x_new,) = _matmul([hid], [(0, w['w_ffn_out'], lead, 0)],
                           [_tile_extra(x, tm2, tn2), _mod_extra(mods[3 * norm_idx + 2], tn2, st.tiles_per_group(tm2))],
                           functools.partial(_ep_residual, 0.5), [F32], tm=tm2, tn=tn2, n_tiles=d // tn2,
                           name='ffn_out', single_buffer_weights=True)
        return x_new

    x = swiglu_ffn(x, 0, 0)

    h = _norm(x, w['norm_g'][layer, 1], mods[4], mods[3], tm=tmn, tiles_per_group=tpgn, out_dtype=BF16)
    tn = _pick(fox_w, (512, 256, 128))
    nt = fox_w // tn
    w_in = w['w_in']

    def proj(weight, lead, col0, n_tiles, out_dtype, rope_it=False, buf=None):
        if rope_it:
            cos, sin_signed = rope
            rows = cos.shape[0]
            extras = [(cos, (tm, head_dim), lambda j, i: (i % (rows // tm), 0)),
                      (sin_signed, (tm, head_dim), lambda j, i: (i % (rows // tm), 0))]
            ep = functools.partial(_ep_rope, head_dim)
        else:
            extras, ep = [], _ep_plain
        (out,) = _matmul([h], [(0, weight, lead, col0)], extras, ep, [out_dtype], tm=tm, tn=tn, n_tiles=n_tiles,
                         name='proj_rope' if rope_it else 'proj',
                         head_major=None if buf is None else (buf, layer, tpg))
        return out

    fq = proj(w_in, (layer,), 0, nt, BF16)
    fk_buf, fv_buf, dk_buf, dv_buf = kv_bufs if kv_bufs is not None else (None,) * 4
    fk = proj(w_in, (layer,), nt, nt, F32, buf=fk_buf)
    fv = proj(w_in, (layer,), 2 * nt, nt, F32, buf=fv_buf)
    w_d = w['w_in_diff']
    dq = proj(w_d, (layer,), 0, nt, BF16, rope_it=True)
    dk = proj(w_d, (layer,), nt, nt, F32, rope_it=True, buf=dk_buf)
    dv = proj(w_d, (layer,), 2 * nt, nt, F32, buf=dv_buf)
    u = proj(w_d, (layer,), 3 * nt, pool_w // tn, F32)
    (z,) = _matmul([h], [(0, w['w_in_forget'], (layer,), 0)], [], _ep_plain, [F32],
                   tm=tm, tn=LANES_V7X, n_tiles=1, name='proj_forget')

    y_fox, y_diff, y_pool, logf, u_state = attention(fq, fk, fv, z, dq, dk, dv, u)

    tng = _pick(d, (512, 256, 128))
    ntg = d // tng
    (gates,) = _matmul([h], [(0, w['w_gate'], (layer,), 0)],
                       [(w['b_gate'].reshape(-1, 1, 3 * d), (None, 1, tng), lambda j, i: (layer, 0, j))],
                       _ep_sigmoid_bias, [BF16], tm=tm, tn=tng, n_tiles=3 * ntg, name='gates')
    (merged,) = _matmul([y_fox, y_diff, y_pool],
                        [(0, w['w_br_fox'], (layer,), 0), (1, w['w_br_diff'], (layer,), 0),
                         (2, w['w_br_pool'], (layer,), 0)],
                        [_tile_extra(gates, tm, tng, b * ntg) for b in range(3)],
                        _ep_merge, [BF16], tm=tm, tn=tng, n_tiles=ntg, name='merge')
    (x,) = _matmul([merged], [(0, w['w_o'], (layer,), 0)],
                   [_tile_extra(x, tm, tng), _mod_extra(mods[5], tng, tpg)],
                   functools.partial(_ep_residual, 1.0), [F32], tm=tm, tn=tng, n_tiles=ntg, name='w_o')

    x = swiglu_ffn(x, 1, 2)
    return x, (fk, fv, logf, dk, dv, u_state)


def kernel(x_prompt, x_sample, c_prompt, c_sample, page_table, cache_fox_k, cache_fox_v, cache_fox_logf, cache_diff_k, cache_diff_v, state_pool, norm_g, w_ada, b_ada, w_ffn_in, w_ffn_out, w_in, b_forget, lambda_qk, subln_g, w_pool, pool_scale, w_br_fox, w_br_diff, w_br_pool, w_gate, b_gate, w_o, final_norm_g):
    n_seq, seq, d = x_prompt.shape
    n_dec, dec_seq, _ = x_sample.shape
    assert dec_seq == 1 and n_dec <= SAMPLE_ROWS
    depth = norm_g.shape[0]
    n_fox, head_dim = cache_fox_k.shape[3], cache_fox_k.shape[4]
    n_diff = cache_diff_v.shape[3]
    fox_w = n_fox * head_dim
    pool_w = w_pool.shape[1] * w_pool.shape[2]
    pool_state = state_pool.shape[2]
    n_pages, page = page_table.shape[1], cache_fox_k.shape[2]
    past_len = n_pages * page
    dims = dict(n_fox=n_fox, n_diff=n_diff, head_dim=head_dim, pool_w=pool_w)
    rows = SAMPLE_ROWS
    assert n_fox <= rows and 2 * n_diff <= rows and n_fox <= LANES_V7X

    off_ff = 3 * fox_w
    off_d = off_ff + n_fox
    weights = dict(norm_g=norm_g, w_ffn_in=w_ffn_in, w_ffn_out=w_ffn_out, w_in=w_in[:, :, :off_ff], w_gate=w_gate,
                   b_gate=b_gate, w_br_fox=w_br_fox, w_br_diff=w_br_diff, w_br_pool=w_br_pool, w_o=w_o,
                   w_in_diff=w_in[:, :, off_d:],
                   w_in_forget=jnp.pad(w_in[:, :, off_ff:off_d], ((0, 0), (0, 0), (0, LANES_V7X - n_fox))))
    b_forget_pad = jnp.pad(b_forget, ((0, 0), (0, LANES_V7X - n_fox)))

    n_c = n_seq + n_dec
    c_rows = -(-n_c // BF16_ROWS_V7X) * BF16_ROWS_V7X
    c_all = jnp.pad(jnp.concatenate([c_prompt, c_sample], axis=0), ((0, c_rows - n_c), (0, 0)))
    tn_ada = _pick(N_MOD * d, (1024, 512, 256, 128))
    mods_p, mods_s = [], []
    for l in range(depth):
        (mod,) = _matmul([c_all], [(0, w_ada, (l,), 0)],
                         [(b_ada.reshape(depth, 1, N_MOD * d), (None, 1, tn_ada), lambda j, i, l=l: (l, 0, j))],
                         _ep_bias, [F32], tm=c_rows, tn=tn_ada, n_tiles=N_MOD * d // tn_ada, name='ada',
                         prologue=_silu_bf16)
        mod = mod.reshape(c_rows, N_MOD, d)
        mods_p.append([mod[:n_seq, k][:, None, :] for k in range(N_MOD)])
        mods_s.append([jnp.pad(mod[n_seq:n_c, k], ((0, rows - n_dec), (0, 0)))[None] for k in range(N_MOD)])

    st_p = _Stream(n_seq * seq, n_seq, seq)
    st_s = _Stream(rows, 1, 1)
    rope_p = _rope_tables(jnp.arange(seq, dtype=jnp.int32), head_dim)
    rope_s = tuple(jnp.broadcast_to(t, (rows, head_dim))
                   for t in _rope_tables(jnp.full((1,), past_len, jnp.int32), head_dim))
    logf_t = jnp.pad(jnp.swapaxes(cache_fox_logf, 2, 3), ((0, 0), (0, 0), (0, rows - n_fox), (0, 0)))
    head_major = (0, 1, 3, 2, 4)
    cache_views = [jnp.transpose(c, head_major) for c in (cache_fox_k, cache_fox_v, cache_diff_k, cache_diff_v)]
    kv_bufs = [jnp.zeros((depth, n_seq, n_fox, seq, head_dim), F32),
               jnp.zeros((depth, n_seq, n_fox, seq, head_dim), F32),
               jnp.zeros((depth, n_seq, 2 * n_diff, seq, head_dim), F32),
               jnp.zeros((depth, n_seq, n_diff, seq, 2 * head_dim), F32)]

    def attention_prompt(l, fq, fk, fv, z, dq, dk, dv, u):
        logf, cum_cols, cum_rows = _logf(z, b_forget_pad[l:l + 1], n_seq=n_seq, seq=seq, n_heads=n_fox,
                                         cumulative=True)
        y_fox = _fox_attention(fq, fk, fv, l, cum_cols, cum_rows, n_seq=n_seq, seq=seq, n_heads=n_fox,
                               head_dim=head_dim)
        lam_init = 0.8 - 0.6 * math.exp(-0.3 * l)
        y_diff = _diff_attention(dq, dk, dv, lambda_qk, subln_g, l, n_seq=n_seq, seq=seq, n_heads=n_diff,
                                 head_dim=head_dim, lam_init=lam_init)
        u3 = u.reshape(n_seq, seq, pool_w)
        y_pool = _pool(u3, w_pool, pool_scale, l).reshape(n_seq * seq, pool_w)
        return y_fox, y_diff, y_pool, logf, u3[:, seq - pool_state:]

    def attention_sample(l, fq, fk, fv, z, dq, dk, dv, u):
        logf, cum_cols, _ = _logf(z, b_forget_pad[l:l + 1], n_seq=1, seq=rows, n_heads=n_fox, cumulative=False)

        def heads(a, n, width):
            a = a[:n_dec].astype(F32).reshape(n_dec, n, width)
            return jnp.pad(a, ((0, 0), (0, -(-n // SUBLANES_V7X) * SUBLANES_V7X - n), (0, 0)))

        def heads16(a, n):
            a = heads(a, n, head_dim)
            return jnp.pad(a, ((0, 0), (0, rows - a.shape[1]), (0, 0)))

        lfn = jnp.pad(cum_cols[:n_dec, :n_fox], ((0, 0), (0, rows - n_fox)))[:, :, None]
        lam_init = 0.8 - 0.6 * math.exp(-0.3 * l)
        o_fox, o_diff = _decode_attention(
            page_table, heads16(fq, n_fox), heads16(fk, n_fox), heads16(fv, n_fox), lfn,
            heads16(dq, 2 * n_diff), heads16(dk, 2 * n_diff), heads(dv, n_diff, 2 * head_dim),
            lambda_qk, subln_g, cache_views[0], cache_views[1], logf_t, cache_views[2], cache_views[3], l,
            n_fox=n_fox, n_diff=n_diff, head_dim=head_dim, lam_init=lam_init)

        def unheads(o, n):
            o = o[:, :n].reshape(n_dec, -1)
            return jnp.pad(o, ((0, rows - n_dec), (0, 0))).astype(BF16)

        u_ext = jnp.concatenate([state_pool[l], u[:n_dec, None, :]], axis=1)
        y_pool = _pool(u_ext, w_pool, pool_scale, l)[:, -1]
        y_pool = jnp.pad(y_pool, ((0, rows - n_dec), (0, 0)))
        return unheads(o_fox, n_fox), unheads(o_diff, n_diff), y_pool, logf, u_ext[:, 1:]

    xp = x_prompt.reshape(n_seq * seq, d)
    xs = jnp.pad(x_sample.reshape(n_dec, d), ((0, rows - n_dec), (0, 0)))
    states_p, states_s = [], []
    for l in range(depth):
        xp, sp = _layer(st_p, xp, mods_p[l], weights, l, rope_p, dims, functools.partial(attention_prompt, l),
                        kv_bufs)
        kv_bufs = [sp[0], sp[1], sp[3], sp[4]]
        xs, ss = _layer(st_s, xs, mods_s[l], weights, l, rope_s, dims, functools.partial(attention_sample, l),
                        None)
        states_p.append(sp)
        states_s.append(ss)

    y_prompt = _norm(xp, final_norm_g, None, None, tm=st_p.tm_norm, tiles_per_group=1, out_dtype=F32)
    y_sample = _norm(xs, final_norm_g, None, None, tm=st_s.tm_norm, tiles_per_group=1, out_dtype=F32)

    def stack_p(i, shape):
        return jnp.stack([s[i].reshape(shape) for s in states_p], axis=0)

    def stack_s(i, shape):
        return jnp.stack([s[i][:n_dec].reshape(shape) for s in states_s], axis=0)

    return (y_prompt.reshape(n_seq, seq, d), y_sample[:n_dec].reshape(n_dec, 1, d),
            jnp.transpose(kv_bufs[0], head_major), jnp.transpose(kv_bufs[1], head_major),
            stack_p(2, (n_seq, seq, n_fox)),
            jnp.transpose(kv_bufs[2], head_major), jnp.transpose(kv_bufs[3], head_major),
            jnp.stack([s[5] for s in states_p], axis=0),
            stack_s(0, (n_dec, 1, n_fox, head_dim)), stack_s(1, (n_dec, 1, n_fox, head_dim)),
            stack_s(2, (n_dec, 1, n_fox)),
            stack_s(3, (n_dec, 1, 2 * n_diff, head_dim)), stack_s(4, (n_dec, 1, n_diff, 2 * head_dim)),
            jnp.stack([s[5] for s in states_s], axis=0))
```

```python
import functools
import math

import jax
import jax.numpy as jnp
from jax import lax
from jax.experimental import pallas as pl
from jax.experimental.pallas import tpu as pltpu

F32 = jnp.float32
BF16 = jnp.bfloat16

LANES_V7X = 128
SUBLANES_V7X = 8
BF16_ROWS_V7X = 16
VMEM_CAP_V7X = 58 * 1024 * 1024

POOL_WINDOWS = (2, 4, 8, 16)
ROPE_THETA = 10000.0
EPS = 1e-6
N_MOD = 9
SAMPLE_ROWS = BF16_ROWS_V7X


def _pick(n, prefs):
    for p in prefs:
        if n % p == 0:
            return p
    return n


def _nbytes(shape, dtype):
    return math.prod(shape) * jnp.dtype(dtype).itemsize


def _params(sem, vmem_estimate):
    limit = min(VMEM_CAP_V7X, int(vmem_estimate * 5 // 4) + (4 << 20))
    return pltpu.CompilerParams(dimension_semantics=sem, vmem_limit_bytes=limit)


def _w_imap(lead, col0, j, i):
    return (*lead, 0, col0 + j)


def _x_imap(j, i):
    return (i, 0)


def _out_imap(j, i):
    return (i, j)


def _mm_kernel(*refs, n_x, n_p, n_e, n_o, n_alias, with_tail, x_of, prologue, epilogue, head_w):
    x_refs, refs = refs[:n_x], refs[n_x:]
    w_refs, refs = refs[:n_p], refs[n_p:]
    e_refs, refs = refs[:n_e], refs[n_e + n_alias:]
    if with_tail:
        xt_refs, refs = refs[:n_x], refs[n_x:]
        et_refs, refs = refs[:n_e], refs[n_e:]
    o_refs, refs = refs[:n_o], refs[n_o:]
    if with_tail:
        ot_refs, refs = refs[:n_o], refs[n_o:]
    wb_refs = refs

    def compute(x_refs, e_refs):
        xs = [x_ref[...] for x_ref in x_refs]
        if prologue is not None:
            xs = [prologue(x) for x in xs]
        accs = [jnp.dot(xs[x_of[p]], wb_refs[p][...], preferred_element_type=F32) for p in range(n_p)]
        return epilogue(accs, [e_ref[...] for e_ref in e_refs])

    @pl.when(pl.program_id(1) == 0)
    def _new_column_tile():
        for w_ref, wb_ref in zip(w_refs, wb_refs):
            wb_ref[...] = w_ref[...].astype(BF16)
        if with_tail:
            for o_ref, o in zip(ot_refs, compute(xt_refs, et_refs)):
                o_ref[...] = o.astype(o_ref.dtype)

    for o_ref, o in zip(o_refs, compute(x_refs, e_refs)):
        if head_w is None:
            o_ref[...] = o.astype(o_ref.dtype)
        else:
            for h in range(o_ref.shape[0]):
                o_ref[h] = o[:, h * head_w:(h + 1) * head_w].astype(o_ref.dtype)


def _matmul(xs, products, extras, epilogue, out_dtypes, *, tm, tn, n_tiles, name, prologue=None, head_major=None,
            single_buffer_weights=False, tail=None):
    m = xs[0].shape[0]
    grid = (n_tiles, m // tm)
    in_specs, vmem = [], 0
    for x in xs:
        in_specs.append(pl.BlockSpec((tm, x.shape[1]), _x_imap))
        vmem += 2 * _nbytes((tm, x.shape[1]), x.dtype)
    scratch = []
    for _, w, lead, col0 in products:
        k = w.shape[-2]
        mode = dict(pipeline_mode=pl.Buffered(1)) if single_buffer_weights else {}
        in_specs.append(pl.BlockSpec((None,) * len(lead) + (k, tn), functools.partial(_w_imap, lead, col0),
                                     **mode))
        scratch.append(pltpu.VMEM((k, tn), BF16))
        vmem += (1 if single_buffer_weights else 2) * _nbytes((k, tn), w.dtype) + _nbytes((k, tn), BF16)
    for arr, block, imap in extras:
        in_specs.append(pl.BlockSpec(block, imap))
        vmem += 2 * _nbytes([b for b in block if b is not None], arr.dtype)
    operands = [*xs, *[p[1] for p in products], *[e[0] for e in extras]]
    aliases, head_w = {}, None
    if head_major is None:
        out_shape = [jax.ShapeDtypeStruct((m, n_tiles * tn), dt) for dt in out_dtypes]
        out_specs = [pl.BlockSpec((tm, tn), _out_imap) for _ in out_dtypes]
    else:
        buf, layer, seq_tiles = head_major
        head_w = buf.shape[-1]
        out_shape = [jax.ShapeDtypeStruct(buf.shape, buf.dtype)]
        out_specs = [pl.BlockSpec((None, None, tn // head_w, tm, head_w),
                                  lambda j, i: (layer, i // seq_tiles, j, i % seq_tiles, 0))]
        in_specs.append(pl.BlockSpec(memory_space=pl.ANY))
        aliases = {len(operands): 0}
        operands.append(buf)
    if tail is not None:
        xs_tail, extras_tail = tail
        rows = xs_tail[0].shape[0]
        for x in xs_tail:
            in_specs.append(pl.BlockSpec((rows, x.shape[1]), lambda j, i: (0, 0)))
            vmem += 2 * _nbytes((rows, x.shape[1]), x.dtype)
        for arr, block, imap in extras_tail:
            in_specs.append(pl.BlockSpec(block, imap))
            vmem += 2 * _nbytes([b for b in block if b is not None], arr.dtype)
        operands += [*xs_tail, *[e[0] for e in extras_tail]]
        out_shape = out_shape + [jax.ShapeDtypeStruct((rows, n_tiles * tn), dt) for dt in out_dtypes]
        out_specs = out_specs + [pl.BlockSpec((rows, tn), lambda j, i: (0, j)) for _ in out_dtypes]
    vmem += sum(2 * _nbytes((tm, tn), dt) for dt in out_dtypes) + 4 * len(products) * _nbytes((tm, tn), F32)
    body = functools.partial(
        _mm_kernel, n_x=len(xs), n_p=len(products), n_e=len(extras), n_o=len(out_dtypes),
        n_alias=len(aliases), with_tail=tail is not None, x_of=tuple(p[0] for p in products),
        prologue=prologue, epilogue=epilogue, head_w=head_w)
    outs = pl.pallas_call(
        body, grid=grid, in_specs=in_specs, out_specs=out_specs, out_shape=out_shape,
        scratch_shapes=scratch, name=name, input_output_aliases=aliases,
        compiler_params=_params(("arbitrary", "arbitrary"), vmem),
    )(*operands)
    return outs if tail is None else (outs[:len(out_dtypes)], outs[len(out_dtypes):])


def _mod_extra(arr, tn, tiles_per_group):
    _, r, _ = arr.shape
    return (arr, (None, r, tn), lambda j, i: (i // tiles_per_group, 0, j))


def _tile_extra(arr, tm, tn, col0=0):
    return (arr, (tm, tn), lambda j, i: (i, col0 + j))


def _tail_mod(arr, tn):
    return (arr, (None, arr.shape[1], tn), lambda j, i: (0, 0, j))


def _tail_tile(arr, tn, col0=0):
    return (arr, (arr.shape[0], tn), lambda j, i: (0, col0 + j))


def _ep_plain(accs, extras):
    return [accs[0]]


def _ep_bias(accs, extras):
    return [accs[0] + extras[0]]


def _ep_swiglu(accs, extras):
    gate, up = accs
    return [gate * jax.nn.sigmoid(gate) * up]


def _ep_sigmoid_bias(accs, extras):
    return [jax.nn.sigmoid(accs[0] + extras[0])]


def _ep_residual(coef, accs, extras):
    x_res, gate = extras
    return [x_res + (coef * gate) * accs[0]]


def _ep_merge(accs, extras):
    out = extras[0].astype(F32) * accs[0]
    for g, a in zip(extras[1:], accs[1:]):
        out = out + g.astype(F32) * a
    return [out]


def _ep_rope(head_dim, accs, extras):
    cos, sin_signed = extras
    acc = accs[0]
    heads = []
    for h in range(acc.shape[1] // head_dim):
        a = acc[:, h * head_dim:(h + 1) * head_dim]
        heads.append(a * cos + pltpu.roll(a, head_dim // 2, 1) * sin_signed)
    return [jnp.concatenate(heads, axis=1) if len(heads) > 1 else heads[0]]


def _silu_bf16(x):
    return (x * jax.nn.sigmoid(x)).astype(BF16)


def _norm_kernel(x_ref, g_ref, *rest, modulated):
    x = x_ref[...]
    y = x * lax.rsqrt(jnp.mean(x * x, axis=-1, keepdims=True) + EPS) * g_ref[...]
    if modulated:
        scale_ref, shift_ref, o_ref = rest
        y = y * (1.0 + scale_ref[...]) + shift_ref[...]
    else:
        (o_ref,) = rest
    o_ref[...] = y.astype(o_ref.dtype)


def _norm(x, g, scale, shift, *, tm, tiles_per_group, out_dtype):
    m, d = x.shape
    in_specs = [pl.BlockSpec((tm, d), lambda i: (i, 0)), pl.BlockSpec((1, d), lambda i: (0, 0))]
    args = [x, g.reshape(1, d)]
    if scale is not None:
        r = scale.shape[1]
        spec = pl.BlockSpec((None, r, d), lambda i: (i // tiles_per_group, 0, 0))
        in_specs += [spec, spec]
        args += [scale, shift]
    vmem = 6 * _nbytes((tm, d), F32)
    return pl.pallas_call(
        functools.partial(_norm_kernel, modulated=scale is not None), name='norm',
        grid=(m // tm,), in_specs=in_specs, out_specs=pl.BlockSpec((tm, d), lambda i: (i, 0)),
        out_shape=jax.ShapeDtypeStruct((m, d), out_dtype),
        compiler_params=_params(("arbitrary",), vmem),
    )(*args)


def _shift_rows(x, k):
    row = lax.broadcasted_iota(jnp.int32, x.shape, 0)
    return jnp.where(row >= k, pltpu.roll(x, k, 0), 0.0)


def _logf_kernel(z_ref, b_ref, logf_ref, cum_ref, cum_t_ref, *, n_heads, seq, cumulative):
    z = z_ref[...] + b_ref[...]
    logf = jnp.minimum(z, 0.0) - jnp.log1p(jnp.exp(-jnp.abs(z)))
    logf_ref[...] = logf[:, :n_heads]
    cum = logf
    if cumulative:
        k = 1
        while k < seq:
            cum = cum + _shift_rows(cum, k)
            k *= 2
    cum_ref[...] = cum
    cum_t_ref[...] = jnp.transpose(cum)[:cum_t_ref.shape[0], :]


def _logf(z, b_pad, *, n_seq, seq, n_heads, cumulative):
    m, lanes = z.shape
    head_rows = BF16_ROWS_V7X
    return pl.pallas_call(
        functools.partial(_logf_kernel, n_heads=n_heads, seq=seq, cumulative=cumulative), name='logf',
        grid=(n_seq,),
        in_specs=[pl.BlockSpec((seq, lanes), lambda b: (b, 0)), pl.BlockSpec((1, lanes), lambda b: (0, 0))],
        out_specs=[pl.BlockSpec((seq, n_heads), lambda b: (b, 0)),
                   pl.BlockSpec((seq, lanes), lambda b: (b, 0)),
                   pl.BlockSpec((None, head_rows, seq), lambda b: (b, 0, 0))],
        out_shape=[jax.ShapeDtypeStruct((m, n_heads), F32), jax.ShapeDtypeStruct((m, lanes), F32),
                   jax.ShapeDtypeStruct((n_seq, head_rows, seq), F32)],
        compiler_params=_params(("arbitrary",), 16 * _nbytes((seq, lanes), F32)),
    )(z, b_pad)


def _causal_mask(s):
    row = lax.broadcasted_iota(jnp.int32, s.shape, 0)
    col = lax.broadcasted_iota(jnp.int32, s.shape, 1)
    return jnp.where(col <= row, s, -jnp.inf)


def _softmax_accumulate(s, v, state):
    m_chunk = jnp.max(s, axis=1, keepdims=True)
    if state is None:
        p = jnp.exp(s - m_chunk)
        return m_chunk, jnp.sum(p, axis=1, keepdims=True), jnp.dot(p.astype(BF16), v, preferred_element_type=F32)
    m, denom, acc = state
    m_new = jnp.maximum(m, m_chunk)
    alpha = jnp.exp(m - m_new)
    p = jnp.exp(s - m_new)
    return (m_new, alpha * denom + jnp.sum(p, axis=1, keepdims=True),
            alpha * acc + jnp.dot(p.astype(BF16), v, preferred_element_type=F32))


def _nt_dot(a, b):
    return lax.dot_general(a, b, (((1,), (1,)), ((), ())), preferred_element_type=F32)


def _fox_kernel(q_ref, k_ref, v_ref, cc_ref, cr_ref, o_ref, kb_ref, vb_ref, *, tq, nq, scale):
    h = pl.program_id(1)
    qi = pl.program_id(2)

    @pl.when(qi == 0)
    def _cast_kv():
        kb_ref[...] = k_ref[...].astype(BF16)
        vb_ref[...] = v_ref[...].astype(BF16)

    q = q_ref[...]
    lane = lax.broadcasted_iota(jnp.int32, cc_ref.shape, 1)
    cum_q = jnp.sum(jnp.where(lane == h, cc_ref[...], 0.0), axis=1, keepdims=True)
    for n in range(nq):
        @pl.when(qi == n)
        def _block(n=n):
            state = None
            for c in range(n + 1):
                keys = slice(c * tq, (c + 1) * tq)
                s = _nt_dot(q, kb_ref[keys, :]) * scale + (cum_q - cr_ref[pl.ds(h, 1), keys])
                if c == n:
                    s = _causal_mask(s)
                state = _softmax_accumulate(s, vb_ref[keys, :], state)
            _, denom, acc = state
            o_ref[...] = (acc / denom).astype(o_ref.dtype)


def _fox_attention(q, k_buf, v_buf, layer, cum_cols, cum_rows, *, n_seq, seq, n_heads, head_dim):
    m = q.shape[0]
    tq = _pick(seq, (512, 256, 128))
    nq = seq // tq
    lanes = cum_cols.shape[1]
    vmem = (4 * _nbytes((seq, head_dim), F32) + 2 * _nbytes((seq, head_dim), BF16)
            + 6 * _nbytes((tq, seq), F32) + 4 * _nbytes((tq, lanes), F32))
    kv_spec = pl.BlockSpec((None, None, None, seq, head_dim), lambda b, h, i: (layer, b, h, 0, 0))
    return pl.pallas_call(
        functools.partial(_fox_kernel, tq=tq, nq=nq, scale=head_dim ** -0.5), name='fox_attn',
        grid=(n_seq, n_heads, nq),
        in_specs=[pl.BlockSpec((tq, head_dim), lambda b, h, i: (b * nq + i, h)), kv_spec, kv_spec,
                  pl.BlockSpec((tq, lanes), lambda b, h, i: (b * nq + i, 0)),
                  pl.BlockSpec((None, cum_rows.shape[1], seq), lambda b, h, i: (b, 0, 0))],
        out_specs=pl.BlockSpec((tq, head_dim), lambda b, h, i: (b * nq + i, h)),
        out_shape=jax.ShapeDtypeStruct((m, n_heads * head_dim), BF16),
        scratch_shapes=[pltpu.VMEM((seq, head_dim), BF16), pltpu.VMEM((seq, head_dim), BF16)],
        compiler_params=_params(("arbitrary", "arbitrary", "arbitrary"), vmem),
    )(q, k_buf, v_buf, cum_cols, cum_rows)


def _diff_lambda(lq, lam_init):
    a = jnp.sum(lq[0:1, :] * lq[1:2, :], axis=1, keepdims=True)
    b = jnp.sum(lq[2:3, :] * lq[3:4, :], axis=1, keepdims=True)
    return jnp.exp(a) - jnp.exp(b) + lam_init


def _sub_norm(o, g, lam_init):
    return o * lax.rsqrt(jnp.mean(o * o, axis=-1, keepdims=True) + EPS) * g * (1.0 - lam_init)


def _diff_kernel(q_ref, k_ref, v_ref, lq_ref, g_ref, o_ref, kb_ref, vb_ref, *, tq, nq, scale, head_dim, lam_init):
    qi = pl.program_id(2)

    @pl.when(qi == 0)
    def _cast_kv():
        kb_ref[...] = k_ref[...].astype(BF16)
        vb_ref[...] = v_ref[...].astype(BF16)

    q = q_ref[...]
    lam = _diff_lambda(lq_ref[...], lam_init)
    for n in range(nq):
        @pl.when(qi == n)
        def _block(n=n):
            states = [None, None]
            for c in range(n + 1):
                keys = slice(c * tq, (c + 1) * tq)
                for comp in range(2):
                    cols = slice(comp * head_dim, (comp + 1) * head_dim)
                    s = _nt_dot(q[:, cols], kb_ref[comp, keys, :]) * scale
                    if c == n:
                        s = _causal_mask(s)
                    states[comp] = _softmax_accumulate(s, vb_ref[keys, :], states[comp])
            (_, d0, acc0), (_, d1, acc1) = states
            o = acc0 / d0 - lam * (acc1 / d1)
            o_ref[...] = _sub_norm(o, g_ref[...], lam_init).astype(o_ref.dtype)


def _diff_attention(q, k_buf, v_buf, lambda_qk, subln_g, layer, *, n_seq, seq, n_heads, head_dim, lam_init):
    m = q.shape[0]
    tq = _pick(seq, (512, 256, 128))
    nq = seq // tq
    width = 2 * head_dim
    vmem = (4 * _nbytes((seq, width), F32) + 2 * _nbytes((seq, width), BF16) + 8 * _nbytes((tq, seq), F32))
    return pl.pallas_call(
        functools.partial(_diff_kernel, tq=tq, nq=nq, scale=head_dim ** -0.5, head_dim=head_dim,
                          lam_init=lam_init), name='diff_attn',
        grid=(n_seq, n_heads, nq),
        in_specs=[pl.BlockSpec((tq, width), lambda b, h, i: (b * nq + i, h)),
                  pl.BlockSpec((None, None, 2, seq, head_dim), lambda b, h, i: (layer, b, h, 0, 0)),
                  pl.BlockSpec((None, None, None, seq, width), lambda b, h, i: (layer, b, h, 0, 0)),
                  pl.BlockSpec((None, 4, head_dim), lambda b, h, i: (layer, 0, 0)),
                  pl.BlockSpec((None, 1, width), lambda b, h, i: (layer, 0, 0))],
        out_specs=pl.BlockSpec((tq, width), lambda b, h, i: (b * nq + i, h)),
        out_shape=jax.ShapeDtypeStruct((m, n_heads * width), BF16),
        scratch_shapes=[pltpu.VMEM((2, seq, head_dim), BF16), pltpu.VMEM((seq, width), BF16)],
        compiler_params=_params(("arbitrary", "arbitrary", "arbitrary"), vmem),
    )(q, k_buf, v_buf, lambda_qk, subln_g.reshape(subln_g.shape[0], 1, width))


def _pool_kernel(u_ref, w_ref, ps_ref, o_ref, *, seq, group_w):
    row = lax.broadcasted_iota(jnp.int32, (seq, 1), 0)
    for g, window in enumerate(POOL_WINDOWS):
        cols = slice(g * group_w, (g + 1) * group_w)
        u = u_ref[:, cols]
        total = u
        k = 1
        while k < window:
            total = total + _shift_rows(total, k)
            k *= 2
        count = jnp.minimum(row + 1, window).astype(F32)
        pooled = total / count - u
        y = jnp.dot(pooled.astype(BF16), w_ref[g].astype(BF16), preferred_element_type=F32)
        o_ref[:, cols] = (y * ps_ref[:, cols]).astype(o_ref.dtype)


def _pool(u, w_pool, pool_scale, layer):
    n_seq, seq, width = u.shape
    group_w = width // len(POOL_WINDOWS)
    return pl.pallas_call(
        functools.partial(_pool_kernel, seq=seq, group_w=group_w), name='pool',
        grid=(n_seq,),
        in_specs=[pl.BlockSpec((None, seq, width), lambda b: (b, 0, 0)),
                  pl.BlockSpec((None,) + w_pool.shape[1:], lambda b: (layer, 0, 0, 0)),
                  pl.BlockSpec((None, 1, width), lambda b: (layer, 0, 0))],
        out_specs=pl.BlockSpec((None, seq, width), lambda b: (b, 0, 0)),
        out_shape=jax.ShapeDtypeStruct((n_seq, seq, width), BF16),
        compiler_params=_params(("arbitrary",), 5 * _nbytes((seq, width), F32)),
    )(u, w_pool, pool_scale.reshape(pool_scale.shape[0], 1, width))


def _split3(x):
    hi = x.astype(BF16)
    r = x - hi.astype(F32)
    mid = r.astype(BF16)
    lo = (r - mid.astype(F32)).astype(BF16)
    return hi, mid, lo


def _row_select(stack_ref, n, rows):
    row = lax.broadcasted_iota(jnp.int32, stack_ref.shape[1:], 0)
    out = jnp.zeros(stack_ref.shape[1:], F32)
    for h in range(n):
        out = jnp.where(row == h, stack_ref[h], out)
    return out


def _decode_kernel(pt_ref, qf_ref, kfn_ref, vfn_ref, lfn_ref, qd_ref, kdn_ref, vdn_ref, lq_ref, g_ref, *rest,
                   n_slots, n_fox, n_diff, head_dim, scale, lam_init):
    del pt_ref
    pages = [rest[5 * k:5 * k + 5] for k in range(n_slots)]
    (of_ref, od_ref, qbf_ref, qbd_ref, mf_ref, lf_ref, accf_ref, cf_ref, md_ref, ld_ref,
     accd_ref) = rest[5 * n_slots:]
    p = pl.program_id(1)
    rows = qf_ref.shape[0]
    row_id = lax.broadcasted_iota(jnp.int32, (rows, head_dim), 0)

    @pl.when(p == 0)
    def _init():
        qf = qf_ref[...]
        qd = qd_ref[...]
        for h in range(n_fox):
            qbf_ref[h] = jnp.where(row_id == h, qf, 0.0).astype(BF16)
            accf_ref[h] = jnp.broadcast_to(vfn_ref[h:h + 1, :], (rows, head_dim))
        for r in range(2 * n_diff):
            qbd_ref[r] = jnp.where(row_id == r, qd, 0.0).astype(BF16)
        for h in range(n_diff):
            accd_ref[h] = jnp.broadcast_to(vdn_ref[h:h + 1, :], (rows, 2 * head_dim))
        mf_ref[...] = jnp.sum(qf.astype(BF16).astype(F32) * kfn_ref[...].astype(BF16).astype(F32),
                              axis=1, keepdims=True) * scale
        md_ref[...] = jnp.sum(qd.astype(BF16).astype(F32) * kdn_ref[...].astype(BF16).astype(F32),
                              axis=1, keepdims=True) * scale
        lf_ref[...] = jnp.ones_like(lf_ref)
        ld_ref[...] = jnp.ones_like(ld_ref)
        cf_ref[...] = lfn_ref[...]

    page = pages[0][2].shape[1]
    later = (lax.broadcasted_iota(jnp.int32, (page, page), 0)
             > lax.broadcasted_iota(jnp.int32, (page, page), 1)).astype(BF16)

    def softmax_step(s, m_ref, l_ref):
        m_new = jnp.maximum(m_ref[...], jnp.max(s, axis=1, keepdims=True))
        alpha = jnp.exp(m_ref[...] - m_new)
        prob = jnp.exp(s - m_new)
        l_ref[...] = alpha * l_ref[...] + jnp.sum(prob, axis=1, keepdims=True)
        m_ref[...] = m_new
        return alpha, prob.astype(BF16)

    def scores(q_stack_ref, k_ref, n):
        s = _nt_dot(q_stack_ref[0], k_ref[0].astype(BF16))
        for r in range(1, n):
            s = s + _nt_dot(q_stack_ref[r], k_ref[r].astype(BF16))
        return s

    def weighted_values(prob, v_refs, h):
        return sum(jnp.dot(prob[:, k * page:(k + 1) * page], v_ref[h].astype(BF16), preferred_element_type=F32)
                   for k, v_ref in enumerate(v_refs))

    decay = cf_ref[...]
    s_fox, s_diff = [], []
    for ck_ref, _, clf_ref, cdk_ref, _ in pages:
        logf_t = clf_ref[...]
        suffix = sum(jnp.dot(t, later, preferred_element_type=F32) for t in _split3(logf_t))
        s_fox.append(scores(qbf_ref, ck_ref, n_fox) * scale + (decay + suffix))
        s_diff.append(scores(qbd_ref, cdk_ref, 2 * n_diff) * scale)
        decay = decay + jnp.sum(logf_t, axis=1, keepdims=True)
    cf_ref[...] = decay

    alpha, prob = softmax_step(jnp.concatenate(s_fox, axis=1), mf_ref, lf_ref)
    for h in range(n_fox):
        accf_ref[h] = alpha * accf_ref[h] + weighted_values(prob, [pg[1] for pg in pages], h)
    alpha, prob = softmax_step(jnp.concatenate(s_diff, axis=1), md_ref, ld_ref)
    for h in range(n_diff):
        accd_ref[h] = alpha * accd_ref[h] + weighted_values(prob, [pg[4] for pg in pages], h)

    @pl.when(p == pl.num_programs(1) - 1)
    def _finish():
        of_ref[...] = _row_select(accf_ref, n_fox, rows) / lf_ref[...]
        lam = _diff_lambda(lq_ref[...], lam_init)
        od_ref[...] = jnp.zeros_like(od_ref)
        for h in range(n_diff):
            a0 = accd_ref[h][2 * h:2 * h + 1, :] / ld_ref[2 * h:2 * h + 1, :]
            a1 = accd_ref[h][2 * h + 1:2 * h + 2, :] / ld_ref[2 * h + 1:2 * h + 2, :]
            od_ref[h:h + 1, :] = _sub_norm(a0 - lam * a1, g_ref[...], lam_init)


def _decode_attention(page_table, qf, kfn, vfn, lfn, qd, kdn, vdn, lambda_qk, subln_g,
                      cache_fox_k, cache_fox_v, logf_t, cache_diff_k, cache_diff_v, layer,
                      *, n_fox, n_diff, head_dim, lam_init):
    n_seq, n_pages = page_table.shape
    rows = qf.shape[1]
    page = cache_fox_k.shape[3]
    width = 2 * head_dim
    vrows = vdn.shape[1]
    n_slots = _pick(n_pages, (4, 2, 1))

    def per_seq(shape):
        return pl.BlockSpec((None,) + shape, lambda b, p, pt: (b, 0, 0))

    def paged(shape, slot):
        return pl.BlockSpec(
            (None, None) + shape,
            lambda b, p, pt: (layer, pt[b, n_pages - 1 - (p * n_slots + slot)]) + (0,) * len(shape))

    in_specs = [per_seq((rows, head_dim)), per_seq((rows, head_dim)), per_seq((rows, head_dim)),
                per_seq((rows, 1)),
                per_seq((rows, head_dim)), per_seq((rows, head_dim)), per_seq((vrows, width)),
                pl.BlockSpec((None, 4, head_dim), lambda b, p, pt: (layer, 0, 0)),
                pl.BlockSpec((None, 1, width), lambda b, p, pt: (layer, 0, 0))]
    caches = []
    for slot in range(n_slots):
        in_specs += [paged((n_fox, page, head_dim), slot), paged((n_fox, page, head_dim), slot),
                     paged((rows, page), slot),
                     paged((2 * n_diff, page, head_dim), slot), paged((n_diff, page, width), slot)]
        caches += [cache_fox_k, cache_fox_v, logf_t, cache_diff_k, cache_diff_v]
    scratch = [pltpu.VMEM((n_fox, rows, head_dim), BF16), pltpu.VMEM((2 * n_diff, rows, head_dim), BF16),
               pltpu.VMEM((rows, 1), F32), pltpu.VMEM((rows, 1), F32),
               pltpu.VMEM((n_fox, rows, head_dim), F32), pltpu.VMEM((rows, 1), F32),
               pltpu.VMEM((rows, 1), F32), pltpu.VMEM((rows, 1), F32),
               pltpu.VMEM((n_diff, rows, width), F32)]
    vmem = 2 * n_slots * (4 * _nbytes((n_fox, page, head_dim), F32) + _nbytes((rows, page), F32)) + (4 << 20)
    grid_spec = pltpu.PrefetchScalarGridSpec(
        num_scalar_prefetch=1, grid=(n_seq, n_pages // n_slots), in_specs=in_specs,
        out_specs=[pl.BlockSpec((None, rows, head_dim), lambda b, p, pt: (b, 0, 0)),
                   pl.BlockSpec((None, vrows, width), lambda b, p, pt: (b, 0, 0))],
        scratch_shapes=scratch)
    return pl.pallas_call(
        functools.partial(_decode_kernel, n_slots=n_slots, n_fox=n_fox, n_diff=n_diff, head_dim=head_dim,
                          scale=head_dim ** -0.5, lam_init=lam_init),
        grid_spec=grid_spec, name='decode_attn',
        out_shape=[jax.ShapeDtypeStruct((n_seq, rows, head_dim), F32),
                   jax.ShapeDtypeStruct((n_seq, vrows, width), F32)],
        compiler_params=_params(("arbitrary", "arbitrary"), vmem),
    )(page_table, qf, kfn, vfn, lfn, qd, kdn, vdn, lambda_qk,
      subln_g.reshape(subln_g.shape[0], 1, width), *caches)


def _rope_tables(pos, head_dim):
    half = head_dim // 2
    inv_freq = ROPE_THETA ** (-jnp.arange(half, dtype=F32) / half)
    ang = pos.astype(F32)[:, None] * inv_freq[None, :]
    cos, sin = jnp.cos(ang), jnp.sin(ang)
    return jnp.concatenate([cos, cos], axis=-1), jnp.concatenate([-sin, sin], axis=-1)


class _Tiling:
    def __init__(self, n_rows, n_groups):
        self.m = n_rows
        self.tm = _pick(n_rows, (1024, 512, 256, 128))
        self.tm_wide_k = _pick(n_rows, (256, 128))
        self.tm_norm = _pick(n_rows, (256, 128))
        self.n_groups = n_groups

    def tiles_per_group(self, tm):
        return max(1, self.m // self.n_groups // tm)


def _layer(st, x, x_t, mods, mods_t, w, layer, rope, rope_t, dims, attention, attention_t, kv_bufs):
    d = x.shape[1]
    tm, tpg = st.tm, st.tiles_per_group(st.tm)
    tmn, tpgn = st.tm_norm, st.tiles_per_group(st.tm_norm)
    rows_t = x_t.shape[0]
    d_ff = w['w_ffn_out'].shape[2]
    n_fox, n_diff, head_dim, pool_w = dims['n_fox'], dims['n_diff'], dims['head_dim'], dims['pool_w']
    fox_w = n_fox * head_dim

    def pre_norm(x, x_t, idx):
        g = w['norm_g'][layer, idx]
        h = _norm(x, g, mods[3 * idx + 1], mods[3 * idx], tm=tmn, tiles_per_group=tpgn, out_dtype=BF16)
        h_t = _norm(x_t, g, mods_t[3 * idx + 1], mods_t[3 * idx], tm=rows_t, tiles_per_group=1, out_dtype=BF16)
        return h, h_t

    def swiglu_ffn(x, x_t, which, norm_idx):
        h, h_t = pre_norm(x, x_t, norm_idx)
        tn = _pick(d_ff, (256, 128))
        lead = (layer, which)
        (hid,), (hid_t,) = _matmul(
            [h], [(0, w['w_ffn_in'], lead, 0), (0, w['w_ffn_in'], lead, d_ff // tn)], [],
            _ep_swiglu, [BF16], tm=tm, tn=tn, n_tiles=d_ff // tn, name='ffn_in', tail=([h_t], []))
        tm2 = st.tm_wide_k
        tn2 = _pick(d, (512, 256, 128))
        gate, gate_t = mods[3 * norm_idx + 2], mods_t[3 * norm_idx + 2]
        (x_new,), (x_new_t,) = _matmul(
            [hid], [(0, w['w_ffn_out'], lead, 0)],
            [_tile_extra(x, tm2, tn2), _mod_extra(gate, tn2, st.tiles_per_group(tm2))],
            functools.partial(_ep_residual, 0.5), [F32], tm=tm2, tn=tn2, n_tiles=d // tn2,
            name='ffn_out', single_buffer_weights=True,
            tail=([hid_t], [_tail_tile(x_t, tn2), _tail_mod(gate_t, tn2)]))
        return x_new, x_new_t

    x, x_t = swiglu_ffn(x, x_t, 0, 0)

    h, h_t = pre_norm(x, x_t, 1)
    tn = _pick(fox_w, (512, 256, 128))
    nt = fox_w // tn
    w_in = w['w_in']

    def proj(weight, lead, col0, n_tiles, out_dtype, rope_it=False, buf=None):
        if rope_it:
            tiles = rope[0].shape[0] // tm
            extras = [(t, (tm, head_dim), lambda j, i: (i % tiles, 0)) for t in rope]
            extras_t = [(t, (rows_t, head_dim), lambda j, i: (0, 0)) for t in rope_t]
            ep = functools.partial(_ep_rope, head_dim)
        else:
            extras, extras_t, ep = [], [], _ep_plain
        (out,), (out_t,) = _matmul(
            [h], [(0, weight, lead, col0)], extras, ep, [out_dtype], tm=tm, tn=tn, n_tiles=n_tiles,
            name='proj_rope' if rope_it else 'proj', head_major=None if buf is None else (buf, layer, tpg),
            tail=([h_t], extras_t))
        return out, out_t

    fk_buf, fv_buf, dk_buf, dv_buf = kv_bufs
    fq, fq_t = proj(w_in, (layer,), 0, nt, BF16)
    fk, fk_t = proj(w_in, (layer,), nt, nt, F32, buf=fk_buf)
    fv, fv_t = proj(w_in, (layer,), 2 * nt, nt, F32, buf=fv_buf)
    w_d = w['w_in_diff']
    dq, dq_t = proj(w_d, (layer,), 0, nt, BF16, rope_it=True)
    dk, dk_t = proj(w_d, (layer,), nt, nt, F32, rope_it=True, buf=dk_buf)
    dv, dv_t = proj(w_d, (layer,), 2 * nt, nt, F32, buf=dv_buf)
    u, u_t = proj(w_d, (layer,), 3 * nt, pool_w // tn, F32)
    (z,), (z_t,) = _matmul([h], [(0, w['w_in_forget'], (layer,), 0)], [], _ep_plain, [F32],
                           tm=tm, tn=LANES_V7X, n_tiles=1, name='proj_forget', tail=([h_t], []))

    y_fox, y_diff, y_pool, logf, u_state = attention(fq, fk, fv, z, dq, dk, dv, u)
    y_fox_t, y_diff_t, y_pool_t, logf_t, u_state_t = attention_t(fq_t, fk_t, fv_t, z_t, dq_t, dk_t, dv_t, u_t)

    tng = _pick(d, (512, 256, 128))
    ntg = d // tng
    bias = (w['b_gate'].reshape(-1, 1, 3 * d), (None, 1, tng), lambda j, i: (layer, 0, j))
    (gates,), (gates_t,) = _matmul([h], [(0, w['w_gate'], (layer,), 0)], [bias],
                                   _ep_sigmoid_bias, [BF16], tm=tm, tn=tng, n_tiles=3 * ntg, name='gates',
                                   tail=([h_t], [bias]))
    (merged,), (merged_t,) = _matmul(
        [y_fox, y_diff, y_pool],
        [(0, w['w_br_fox'], (layer,), 0), (1, w['w_br_diff'], (layer,), 0), (2, w['w_br_pool'], (layer,), 0)],
        [_tile_extra(gates, tm, tng, b * ntg) for b in range(3)],
        _ep_merge, [BF16], tm=tm, tn=tng, n_tiles=ntg, name='merge',
        tail=([y_fox_t, y_diff_t, y_pool_t], [_tail_tile(gates_t, tng, b * ntg) for b in range(3)]))
    (x,), (x_t,) = _matmul([merged], [(0, w['w_o'], (layer,), 0)],
                           [_tile_extra(x, tm, tng), _mod_extra(mods[5], tng, tpg)],
                           functools.partial(_ep_residual, 1.0), [F32], tm=tm, tn=tng, n_tiles=ntg, name='w_o',
                           tail=([merged_t], [_tail_tile(x_t, tng), _tail_mod(mods_t[5], tng)]))

    x, x_t = swiglu_ffn(x, x_t, 1, 2)
    return x, x_t, (fk, fv, logf, dk, dv, u_state), (fk_t, fv_t, logf_t, dk_t, dv_t, u_state_t)


def kernel(x_prompt, x_sample, c_prompt, c_sample, page_table, cache_fox_k, cache_fox_v, cache_fox_logf, cache_diff_k, cache_diff_v, state_pool, norm_g, w_ada, b_ada, w_ffn_in, w_ffn_out, w_in, b_forget, lambda_qk, subln_g, w_pool, pool_scale, w_br_fox, w_br_diff, w_br_pool, w_gate, b_gate, w_o, final_norm_g):
    n_seq, seq, d = x_prompt.shape
    n_dec, dec_seq, _ = x_sample.shape
    assert dec_seq == 1 and n_dec <= SAMPLE_ROWS
    depth = norm_g.shape[0]
    n_fox, head_dim = cache_fox_k.shape[3], cache_fox_k.shape[4]
    n_diff = cache_diff_v.shape[3]
    fox_w = n_fox * head_dim
    pool_w = w_pool.shape[1] * w_pool.shape[2]
    pool_state = state_pool.shape[2]
    n_pages, page = page_table.shape[1], cache_fox_k.shape[2]
    past_len = n_pages * page
    dims = dict(n_fox=n_fox, n_diff=n_diff, head_dim=head_dim, pool_w=pool_w)
    rows = SAMPLE_ROWS
    assert n_fox <= rows and 2 * n_diff <= rows and n_fox <= LANES_V7X

    off_ff = 3 * fox_w
    off_d = off_ff + n_fox
    weights = dict(norm_g=norm_g, w_ffn_in=w_ffn_in, w_ffn_out=w_ffn_out, w_in=w_in[:, :, :off_ff], w_gate=w_gate,
                   b_gate=b_gate, w_br_fox=w_br_fox, w_br_diff=w_br_diff, w_br_pool=w_br_pool, w_o=w_o,
                   w_in_diff=w_in[:, :, off_d:],
                   w_in_forget=jnp.pad(w_in[:, :, off_ff:off_d], ((0, 0), (0, 0), (0, LANES_V7X - n_fox))))
    b_forget_pad = jnp.pad(b_forget, ((0, 0), (0, LANES_V7X - n_fox)))

    n_c = n_seq + n_dec
    c_rows = -(-n_c // BF16_ROWS_V7X) * BF16_ROWS_V7X
    c_all = jnp.pad(jnp.concatenate([c_prompt, c_sample], axis=0), ((0, c_rows - n_c), (0, 0)))
    tn_ada = _pick(N_MOD * d, (1024, 512, 256, 128))
    mods_p, mods_s = [], []
    for l in range(depth):
        (mod,) = _matmul([c_all], [(0, w_ada, (l,), 0)],
                         [(b_ada.reshape(depth, 1, N_MOD * d), (None, 1, tn_ada), lambda j, i, l=l: (l, 0, j))],
                         _ep_bias, [F32], tm=c_rows, tn=tn_ada, n_tiles=N_MOD * d // tn_ada, name='ada',
                         prologue=_silu_bf16)
        mod = mod.reshape(c_rows, N_MOD, d)
        mods_p.append([mod[:n_seq, k][:, None, :] for k in range(N_MOD)])
        mods_s.append([jnp.pad(mod[n_seq:n_c, k], ((0, rows - n_dec), (0, 0)))[None] for k in range(N_MOD)])

    st_p = _Tiling(n_seq * seq, n_seq)
    rope_p = _rope_tables(jnp.arange(seq, dtype=jnp.int32), head_dim)
    rope_s = tuple(jnp.broadcast_to(t, (rows, head_dim))
                   for t in _rope_tables(jnp.full((1,), past_len, jnp.int32), head_dim))
    logf_t = jnp.pad(jnp.swapaxes(cache_fox_logf, 2, 3), ((0, 0), (0, 0), (0, rows - n_fox), (0, 0)))
    head_major = (0, 1, 3, 2, 4)
    cache_views = [jnp.transpose(c, head_major) for c in (cache_fox_k, cache_fox_v, cache_diff_k, cache_diff_v)]
    kv_bufs = [jnp.zeros((depth, n_seq, n_fox, seq, head_dim), F32),
               jnp.zeros((depth, n_seq, n_fox, seq, head_dim), F32),
               jnp.zeros((depth, n_seq, 2 * n_diff, seq, head_dim), F32),
               jnp.zeros((depth, n_seq, n_diff, seq, 2 * head_dim), F32)]

    def attention_prompt(l, fq, fk, fv, z, dq, dk, dv, u):
        logf, cum_cols, cum_rows = _logf(z, b_forget_pad[l:l + 1], n_seq=n_seq, seq=seq, n_heads=n_fox,
                                         cumulative=True)
        y_fox = _fox_attention(fq, fk, fv, l, cum_cols, cum_rows, n_seq=n_seq, seq=seq, n_heads=n_fox,
                               head_dim=head_dim)
        lam_init = 0.8 - 0.6 * math.exp(-0.3 * l)
        y_diff = _diff_attention(dq, dk, dv, lambda_qk, subln_g, l, n_seq=n_seq, seq=seq, n_heads=n_diff,
                                 head_dim=head_dim, lam_init=lam_init)
        u3 = u.reshape(n_seq, seq, pool_w)
        y_pool = _pool(u3, w_pool, pool_scale, l).reshape(n_seq * seq, pool_w)
        return y_fox, y_diff, y_pool, logf, u3[:, seq - pool_state:]

    def attention_sample(l, fq, fk, fv, z, dq, dk, dv, u):
        logf, cum_cols, _ = _logf(z, b_forget_pad[l:l + 1], n_seq=1, seq=rows, n_heads=n_fox, cumulative=False)

        def heads(a, n, width):
            a = a[:n_dec].astype(F32).reshape(n_dec, n, width)
            return jnp.pad(a, ((0, 0), (0, -(-n // SUBLANES_V7X) * SUBLANES_V7X - n), (0, 0)))

        def heads16(a, n):
            a = heads(a, n, head_dim)
            return jnp.pad(a, ((0, 0), (0, rows - a.shape[1]), (0, 0)))

        lfn = jnp.pad(cum_cols[:n_dec, :n_fox], ((0, 0), (0, rows - n_fox)))[:, :, None]
        lam_init = 0.8 - 0.6 * math.exp(-0.3 * l)
        o_fox, o_diff = _decode_attention(
            page_table, heads16(fq, n_fox), heads16(fk, n_fox), heads16(fv, n_fox), lfn,
            heads16(dq, 2 * n_diff), heads16(dk, 2 * n_diff), heads(dv, n_diff, 2 * head_dim),
            lambda_qk, subln_g, cache_views[0], cache_views[1], logf_t, cache_views[2], cache_views[3], l,
            n_fox=n_fox, n_diff=n_diff, head_dim=head_dim, lam_init=lam_init)

        def unheads(o, n):
            o = o[:, :n].reshape(n_dec, -1)
            return jnp.pad(o, ((0, rows - n_dec), (0, 0))).astype(BF16)

        u_ext = jnp.concatenate([state_pool[l], u[:n_dec, None, :]], axis=1)
        y_pool = _pool(u_ext, w_pool, pool_scale, l)[:, -1]
        y_pool = jnp.pad(y_pool, ((0, rows - n_dec), (0, 0)))
        return unheads(o_fox, n_fox), unheads(o_diff, n_diff), y_pool, logf, u_ext[:, 1:]

    xp = x_prompt.reshape(n_seq * seq, d)
    xs = jnp.pad(x_sample.reshape(n_dec, d), ((0, rows - n_dec), (0, 0)))
    states_p, states_s = [], []
    for l in range(depth):
        xp, xs, sp, ss = _layer(st_p, xp, xs, mods_p[l], mods_s[l], weights, l, rope_p, rope_s, dims,
                                functools.partial(attention_prompt, l), functools.partial(attention_sample, l),
                                kv_bufs)
        kv_bufs = [sp[0], sp[1], sp[3], sp[4]]
        states_p.append(sp)
        states_s.append(ss)

    y_prompt = _norm(xp, final_norm_g, None, None, tm=st_p.tm_norm, tiles_per_group=1, out_dtype=F32)
    y_sample = _norm(xs, final_norm_g, None, None, tm=rows, tiles_per_group=1, out_dtype=F32)

    def stack_p(i, shape):
        return jnp.stack([s[i].reshape(shape) for s in states_p], axis=0)

    def stack_s(i, shape):
        return jnp.stack([s[i][:n_dec].reshape(shape) for s in states_s], axis=0)

    return (y_prompt.reshape(n_seq, seq, d), y_sample[:n_dec].reshape(n_dec, 1, d),
            jnp.transpose(kv_bufs[0], head_major), jnp.transpose(kv_bufs[1], head_major),
            stack_p(2, (n_seq, seq, n_fox)),
            jnp.transpose(kv_bufs[2], head_major), jnp.transpose(kv_bufs[3], head_major),
            jnp.stack([s[5] for s in states_p], axis=0),
            stack_s(0, (n_dec, 1, n_fox, head_dim)), stack_s(1, (n_dec, 1, n_fox, head_dim)),
            stack_s(2, (n_dec, 1, n_fox)),
            stack_s(3, (n_dec, 1, 2 * n_diff, head_dim)), stack_s(4, (n_dec, 1, n_diff, 2 * head_dim)),
            jnp.stack([s[5] for s in states_s], axis=0))
```

```python
import functools
import math

import jax
import jax.numpy as jnp
from jax import lax
from jax.experimental import pallas as pl
from jax.experimental.pallas import tpu as pltpu

F32 = jnp.float32
BF16 = jnp.bfloat16

LANES_V7X = 128
SUBLANES_V7X = 8
BF16_ROWS_V7X = 16
VMEM_CAP_V7X = 58 * 1024 * 1024

POOL_WINDOWS = (2, 4, 8, 16)
ROPE_THETA = 10000.0
EPS = 1e-6
N_MOD = 9
SAMPLE_ROWS = BF16_ROWS_V7X
X_RING_SLOTS = 3


def _pick(n, prefs):
    for p in prefs:
        if n % p == 0:
            return p
    return n


def _nbytes(shape, dtype):
    return math.prod(shape) * jnp.dtype(dtype).itemsize


def _params(sem, vmem_estimate):
    limit = min(VMEM_CAP_V7X, int(vmem_estimate * 5 // 4) + (4 << 20))
    return pltpu.CompilerParams(dimension_semantics=sem, vmem_limit_bytes=limit)


def _w_imap(lead, col0, j, i):
    return (*lead, 0, col0 + j)


def _x_imap(j, i):
    return (i, 0)


def _out_imap(j, i):
    return (i, j)


def _mm_kernel(*refs, n_x, n_p, n_e, n_o, n_alias, with_tail, x_of, prologue, epilogue, head_w, ring_tm):
    x_refs, refs = refs[:n_x], refs[n_x:]
    w_refs, refs = refs[:n_p], refs[n_p:]
    e_refs, refs = refs[:n_e], refs[n_e + n_alias:]
    if with_tail:
        xt_refs, refs = refs[:n_x], refs[n_x:]
        et_refs, refs = refs[:n_e], refs[n_e:]
    o_refs, refs = refs[:n_o], refs[n_o:]
    if with_tail:
        ot_refs, refs = refs[:n_o], refs[n_o:]
    if ring_tm is not None:
        xbuf_refs, sem_ref, refs = refs[:n_x], refs[n_x], refs[n_x + 1:]
        n_i = pl.num_programs(1)
        n_steps = pl.num_programs(0) * n_i
        step = pl.program_id(0) * n_i + pl.program_id(1)

        def tile_copy(k, at_step):
            row = pl.multiple_of((at_step % n_i) * ring_tm, ring_tm)
            slot = at_step % X_RING_SLOTS
            return pltpu.make_async_copy(x_refs[k].at[pl.ds(row, ring_tm), :], xbuf_refs[k].at[slot],
                                         sem_ref.at[k, slot])

        @pl.when(step == 0)
        def _prime():
            for ahead in range(X_RING_SLOTS - 1):
                for k in range(n_x):
                    tile_copy(k, ahead).start()

        @pl.when(step + X_RING_SLOTS - 1 < n_steps)
        def _prefetch():
            for k in range(n_x):
                tile_copy(k, step + X_RING_SLOTS - 1).start()

        for k in range(n_x):
            tile_copy(k, step).wait()
        x_refs = [xbuf_refs[k].at[step % X_RING_SLOTS] for k in range(n_x)]
    wb_refs = refs

    def compute(x_refs, e_refs):
        xs = [x_ref[...] for x_ref in x_refs]
        if prologue is not None:
            xs = [prologue(x) for x in xs]
        accs = [jnp.dot(xs[x_of[p]], wb_refs[p][...], preferred_element_type=F32) for p in range(n_p)]
        return epilogue(accs, [e_ref[...] for e_ref in e_refs])

    @pl.when(pl.program_id(1) == 0)
    def _new_column_tile():
        for w_ref, wb_ref in zip(w_refs, wb_refs):
            wb_ref[...] = w_ref[...].astype(BF16)
        if with_tail:
            for o_ref, o in zip(ot_refs, compute(xt_refs, et_refs)):
                o_ref[...] = o.astype(o_ref.dtype)

    for o_ref, o in zip(o_refs, compute(x_refs, e_refs)):
        if head_w is None:
            o_ref[...] = o.astype(o_ref.dtype)
        else:
            for h in range(o_ref.shape[0]):
                o_ref[h] = o[:, h * head_w:(h + 1) * head_w].astype(o_ref.dtype)


def _matmul(xs, products, extras, epilogue, out_dtypes, *, tm, tn, n_tiles, name, prologue=None, head_major=None,
            single_buffer_weights=False, tail=None):
    m = xs[0].shape[0]
    grid = (n_tiles, m // tm)
    in_specs, vmem = [], 0
    ring = m // tm > 1
    scratch = []
    for x in xs:
        if ring:
            in_specs.append(pl.BlockSpec(memory_space=pl.ANY))
            scratch.append(pltpu.VMEM((X_RING_SLOTS, tm, x.shape[1]), x.dtype))
        else:
            in_specs.append(pl.BlockSpec((tm, x.shape[1]), _x_imap))
        vmem += (X_RING_SLOTS if ring else 2) * _nbytes((tm, x.shape[1]), x.dtype)
    if ring:
        scratch.append(pltpu.SemaphoreType.DMA((len(xs), X_RING_SLOTS)))
    for _, w, lead, col0 in products:
        k = w.shape[-2]
        mode = dict(pipeline_mode=pl.Buffered(1)) if single_buffer_weights else {}
        in_specs.append(pl.BlockSpec((None,) * len(lead) + (k, tn), functools.partial(_w_imap, lead, col0),
                                     **mode))
        scratch.append(pltpu.VMEM((k, tn), BF16))
        vmem += (1 if single_buffer_weights else 2) * _nbytes((k, tn), w.dtype) + _nbytes((k, tn), BF16)
    for arr, block, imap in extras:
        in_specs.append(pl.BlockSpec(block, imap))
        vmem += 2 * _nbytes([b for b in block if b is not None], arr.dtype)
    operands = [*xs, *[p[1] for p in products], *[e[0] for e in extras]]
    aliases, head_w = {}, None
    if head_major is None:
        out_shape = [jax.ShapeDtypeStruct((m, n_tiles * tn), dt) for dt in out_dtypes]
        out_specs = [pl.BlockSpec((tm, tn), _out_imap) for _ in out_dtypes]
    else:
        buf, layer, seq_tiles = head_major
        head_w = buf.shape[-1]
        out_shape = [jax.ShapeDtypeStruct(buf.shape, buf.dtype)]
        out_specs = [pl.BlockSpec((None, None, tn // head_w, tm, head_w),
                                  lambda j, i: (layer, i // seq_tiles, j, i % seq_tiles, 0))]
        in_specs.append(pl.BlockSpec(memory_space=pl.ANY))
        aliases = {len(operands): 0}
        operands.append(buf)
    if tail is not None:
        xs_tail, extras_tail = tail
        rows = xs_tail[0].shape[0]
        for x in xs_tail:
            in_specs.append(pl.BlockSpec((rows, x.shape[1]), lambda j, i: (0, 0)))
            vmem += 2 * _nbytes((rows, x.shape[1]), x.dtype)
        for arr, block, imap in extras_tail:
            in_specs.append(pl.BlockSpec(block, imap))
            vmem += 2 * _nbytes([b for b in block if b is not None], arr.dtype)
        operands += [*xs_tail, *[e[0] for e in extras_tail]]
        out_shape = out_shape + [jax.ShapeDtypeStruct((rows, n_tiles * tn), dt) for dt in out_dtypes]
        out_specs = out_specs + [pl.BlockSpec((rows, tn), lambda j, i: (0, j)) for _ in out_dtypes]
    vmem += sum(2 * _nbytes((tm, tn), dt) for dt in out_dtypes) + 4 * len(products) * _nbytes((tm, tn), F32)
    body = functools.partial(
        _mm_kernel, n_x=len(xs), n_p=len(products), n_e=len(extras), n_o=len(out_dtypes),
        n_alias=len(aliases), with_tail=tail is not None, x_of=tuple(p[0] for p in products),
        prologue=prologue, epilogue=epilogue, head_w=head_w, ring_tm=tm if ring else None)
    outs = pl.pallas_call(
        body, grid=grid, in_specs=in_specs, out_specs=out_specs, out_shape=out_shape,
        scratch_shapes=scratch, name=name, input_output_aliases=aliases,
        compiler_params=_params(("arbitrary", "arbitrary"), vmem),
    )(*operands)
    return outs if tail is None else (outs[:len(out_dtypes)], outs[len(out_dtypes):])


def _mod_extra(arr, tn, tiles_per_group):
    _, r, _ = arr.shape
    return (arr, (None, r, tn), lambda j, i: (i // tiles_per_group, 0, j))


def _tile_extra(arr, tm, tn, col0=0):
    return (arr, (tm, tn), lambda j, i: (i, col0 + j))


def _tail_mod(arr, tn):
    return (arr, (None, arr.shape[1], tn), lambda j, i: (0, 0, j))


def _tail_tile(arr, tn, col0=0):
    return (arr, (arr.shape[0], tn), lambda j, i: (0, col0 + j))


def _ep_plain(accs, extras):
    return [accs[0]]


def _ep_bias(accs, extras):
    return [accs[0] + extras[0]]


def _ep_swiglu(accs, extras):
    gate, up = accs
    return [gate * jax.nn.sigmoid(gate) * up]


def _ep_sigmoid_bias(accs, extras):
    return [jax.nn.sigmoid(accs[0] + extras[0])]


def _ep_residual(coef, accs, extras):
    x_res, gate = extras
    return [x_res + (coef * gate) * accs[0]]


def _ep_merge(accs, extras):
    out = extras[0].astype(F32) * accs[0]
    for g, a in zip(extras[1:], accs[1:]):
        out = out + g.astype(F32) * a
    return [out]


def _ep_rope(head_dim, accs, extras):
    cos, sin_signed = extras
    acc = accs[0]
    heads = []
    for h in range(acc.shape[1] // head_dim):
        a = acc[:, h * head_dim:(h + 1) * head_dim]
        heads.append(a * cos + pltpu.roll(a, head_dim // 2, 1) * sin_signed)
    return [jnp.concatenate(heads, axis=1) if len(heads) > 1 else heads[0]]


def _silu_bf16(x):
    return (x * jax.nn.sigmoid(x)).astype(BF16)


def _norm_kernel(x_ref, g_ref, *rest, modulated):
    x = x_ref[...]
    y = x * lax.rsqrt(jnp.mean(x * x, axis=-1, keepdims=True) + EPS) * g_ref[...]
    if modulated:
        scale_ref, shift_ref, o_ref = rest
        y = y * (1.0 + scale_ref[...]) + shift_ref[...]
    else:
        (o_ref,) = rest
    o_ref[...] = y.astype(o_ref.dtype)


def _norm(x, g, scale, shift, *, tm, tiles_per_group, out_dtype):
    m, d = x.shape
    in_specs = [pl.BlockSpec((tm, d), lambda i: (i, 0)), pl.BlockSpec((1, d), lambda i: (0, 0))]
    args = [x, g.reshape(1, d)]
    if scale is not None:
        r = scale.shape[1]
        spec = pl.BlockSpec((None, r, d), lambda i: (i // tiles_per_group, 0, 0))
        in_specs += [spec, spec]
        args += [scale, shift]
    vmem = 6 * _nbytes((tm, d), F32)
    return pl.pallas_call(
        functools.partial(_norm_kernel, modulated=scale is not None), name='norm',
        grid=(m // tm,), in_specs=in_specs, out_specs=pl.BlockSpec((tm, d), lambda i: (i, 0)),
        out_shape=jax.ShapeDtypeStruct((m, d), out_dtype),
        compiler_params=_params(("arbitrary",), vmem),
    )(*args)


def _shift_rows(x, k):
    row = lax.broadcasted_iota(jnp.int32, x.shape, 0)
    return jnp.where(row >= k, pltpu.roll(x, k, 0), 0.0)


def _logf_kernel(z_ref, b_ref, logf_ref, cum_ref, cum_t_ref, *, n_heads, seq, cumulative):
    z = z_ref[...] + b_ref[...]
    logf = jnp.minimum(z, 0.0) - jnp.log1p(jnp.exp(-jnp.abs(z)))
    logf_ref[...] = logf[:, :n_heads]
    cum = logf
    if cumulative:
        k = 1
        while k < seq:
            cum = cum + _shift_rows(cum, k)
            k *= 2
    cum_ref[...] = cum
    cum_t_ref[...] = jnp.transpose(cum)[:cum_t_ref.shape[0], :]


def _logf(z, b_pad, *, n_seq, seq, n_heads, cumulative):
    m, lanes = z.shape
    head_rows = BF16_ROWS_V7X
    return pl.pallas_call(
        functools.partial(_logf_kernel, n_heads=n_heads, seq=seq, cumulative=cumulative), name='logf',
        grid=(n_seq,),
        in_specs=[pl.BlockSpec((seq, lanes), lambda b: (b, 0)), pl.BlockSpec((1, lanes), lambda b: (0, 0))],
        out_specs=[pl.BlockSpec((seq, n_heads), lambda b: (b, 0)),
                   pl.BlockSpec((seq, lanes), lambda b: (b, 0)),
                   pl.BlockSpec((None, head_rows, seq), lambda b: (b, 0, 0))],
        out_shape=[jax.ShapeDtypeStruct((m, n_heads), F32), jax.ShapeDtypeStruct((m, lanes), F32),
                   jax.ShapeDtypeStruct((n_seq, head_rows, seq), F32)],
        compiler_params=_params(("arbitrary",), 16 * _nbytes((seq, lanes), F32)),
    )(z, b_pad)


def _causal_mask(s):
    row = lax.broadcasted_iota(jnp.int32, s.shape, 0)
    col = lax.broadcasted_iota(jnp.int32, s.shape, 1)
    return jnp.where(col <= row, s, -jnp.inf)


def _softmax_accumulate(s, v, state):
    m_chunk = jnp.max(s, axis=1, keepdims=True)
    if state is None:
        p = jnp.exp(s - m_chunk)
        return m_chunk, jnp.sum(p, axis=1, keepdims=True), jnp.dot(p.astype(BF16), v, preferred_element_type=F32)
    m, denom, acc = state
    m_new = jnp.maximum(m, m_chunk)
    alpha = jnp.exp(m - m_new)
    p = jnp.exp(s - m_new)
    return (m_new, alpha * denom + jnp.sum(p, axis=1, keepdims=True),
            alpha * acc + jnp.dot(p.astype(BF16), v, preferred_element_type=F32))


def _nt_dot(a, b):
    return lax.dot_general(a, b, (((1,), (1,)), ((), ())), preferred_element_type=F32)


def _fox_kernel(q_ref, k_ref, v_ref, cc_ref, cr_ref, o_ref, kb_ref, vb_ref, *, tq, nq, scale):
    h = pl.program_id(1)
    qi = pl.program_id(2)

    @pl.when(qi == 0)
    def _cast_kv():
        kb_ref[...] = k_ref[...].astype(BF16)
        vb_ref[...] = v_ref[...].astype(BF16)

    q = q_ref[...]
    lane = lax.broadcasted_iota(jnp.int32, cc_ref.shape, 1)
    cum_q = jnp.sum(jnp.where(lane == h, cc_ref[...], 0.0), axis=1, keepdims=True)
    for n in range(nq):
        @pl.when(qi == n)
        def _block(n=n):
            state = None
            for c in range(n + 1):
                keys = slice(c * tq, (c + 1) * tq)
                s = _nt_dot(q, kb_ref[keys, :]) * scale + (cum_q - cr_ref[pl.ds(h, 1), keys])
                if c == n:
                    s = _causal_mask(s)
                state = _softmax_accumulate(s, vb_ref[keys, :], state)
            _, denom, acc = state
            o_ref[...] = (acc / denom).astype(o_ref.dtype)


def _fox_attention(q, k_buf, v_buf, layer, cum_cols, cum_rows, *, n_seq, seq, n_heads, head_dim):
    m = q.shape[0]
    tq = _pick(seq, (512, 256, 128))
    nq = seq // tq
    lanes = cum_cols.shape[1]
    vmem = (4 * _nbytes((seq, head_dim), F32) + 2 * _nbytes((seq, head_dim), BF16)
            + 6 * _nbytes((tq, seq), F32) + 4 * _nbytes((tq, lanes), F32))
    kv_spec = pl.BlockSpec((None, None, None, seq, head_dim), lambda b, h, i: (layer, b, h, 0, 0))
    return pl.pallas_call(
        functools.partial(_fox_kernel, tq=tq, nq=nq, scale=head_dim ** -0.5), name='fox_attn',
        grid=(n_seq, n_heads, nq),
        in_specs=[pl.BlockSpec((tq, head_dim), lambda b, h, i: (b * nq + i, h)), kv_spec, kv_spec,
                  pl.BlockSpec((tq, lanes), lambda b, h, i: (b * nq + i, 0)),
                  pl.BlockSpec((None, cum_rows.shape[1], seq), lambda b, h, i: (b, 0, 0))],
        out_specs=pl.BlockSpec((tq, head_dim), lambda b, h, i: (b * nq + i, h)),
        out_shape=jax.ShapeDtypeStruct((m, n_heads * head_dim), BF16),
        scratch_shapes=[pltpu.VMEM((seq, head_dim), BF16), pltpu.VMEM((seq, head_dim), BF16)],
        compiler_params=_params(("arbitrary", "arbitrary", "arbitrary"), vmem),
    )(q, k_buf, v_buf, cum_cols, cum_rows)


def _diff_lambda(lq, lam_init):
    a = jnp.sum(lq[0:1, :] * lq[1:2, :], axis=1, keepdims=True)
    b = jnp.sum(lq[2:3, :] * lq[3:4, :], axis=1, keepdims=True)
    return jnp.exp(a) - jnp.exp(b) + lam_init


def _sub_norm(o, g, lam_init):
    return o * lax.rsqrt(jnp.mean(o * o, axis=-1, keepdims=True) + EPS) * g * (1.0 - lam_init)


def _diff_kernel(q_ref, k_ref, v_ref, lq_ref, g_ref, o_ref, kb_ref, vb_ref, *, tq, nq, scale, head_dim, lam_init):
    qi = pl.program_id(2)

    @pl.when(qi == 0)
    def _cast_kv():
        kb_ref[...] = k_ref[...].astype(BF16)
        vb_ref[...] = v_ref[...].astype(BF16)

    q = q_ref[...]
    lam = _diff_lambda(lq_ref[...], lam_init)
    for n in range(nq):
        @pl.when(qi == n)
        def _block(n=n):
            states = [None, None]
            for c in range(n + 1):
                keys = slice(c * tq, (c + 1) * tq)
                for comp in range(2):
                    cols = slice(comp * head_dim, (comp + 1) * head_dim)
                    s = _nt_dot(q[:, cols], kb_ref[comp, keys, :]) * scale
                    if c == n:
                        s = _causal_mask(s)
                    states[comp] = _softmax_accumulate(s, vb_ref[keys, :], states[comp])
            (_, d0, acc0), (_, d1, acc1) = states
            o = acc0 / d0 - lam * (acc1 / d1)
            o_ref[...] = _sub_norm(o, g_ref[...], lam_init).astype(o_ref.dtype)


def _diff_attention(q, k_buf, v_buf, lambda_qk, subln_g, layer, *, n_seq, seq, n_heads, head_dim, lam_init):
    m = q.shape[0]
    tq = _pick(seq, (512, 256, 128))
    nq = seq // tq
    width = 2 * head_dim
    vmem = (4 * _nbytes((seq, width), F32) + 2 * _nbytes((seq, width), BF16) + 8 * _nbytes((tq, seq), F32))
    return pl.pallas_call(
        functools.partial(_diff_kernel, tq=tq, nq=nq, scale=head_dim ** -0.5, head_dim=head_dim,
                          lam_init=lam_init), name='diff_attn',
        grid=(n_seq, n_heads, nq),
        in_specs=[pl.BlockSpec((tq, width), lambda b, h, i: (b * nq + i, h)),
                  pl.BlockSpec((None, None, 2, seq, head_dim), lambda b, h, i: (layer, b, h, 0, 0)),
                  pl.BlockSpec((None, None, None, seq, width), lambda b, h, i: (layer, b, h, 0, 0)),
                  pl.BlockSpec((None, 4, head_dim), lambda b, h, i: (layer, 0, 0)),
                  pl.BlockSpec((None, 1, width), lambda b, h, i: (layer, 0, 0))],
        out_specs=pl.BlockSpec((tq, width), lambda b, h, i: (b * nq + i, h)),
        out_shape=jax.ShapeDtypeStruct((m, n_heads * width), BF16),
        scratch_shapes=[pltpu.VMEM((2, seq, head_dim), BF16), pltpu.VMEM((seq, width), BF16)],
        compiler_params=_params(("arbitrary", "arbitrary", "arbitrary"), vmem),
    )(q, k_buf, v_buf, lambda_qk, subln_g.reshape(subln_g.shape[0], 1, width))


def _pool_kernel(u_ref, w_ref, ps_ref, o_ref, *, seq, group_w):
    row = lax.broadcasted_iota(jnp.int32, (seq, 1), 0)
    for g, window in enumerate(POOL_WINDOWS):
        cols = slice(g * group_w, (g + 1) * group_w)
        u = u_ref[:, cols]
        total = u
        k = 1
        while k < window:
            total = total + _shift_rows(total, k)
            k *= 2
        count = jnp.minimum(row + 1, window).astype(F32)
        pooled = total / count - u
        y = jnp.dot(pooled.astype(BF16), w_ref[g].astype(BF16), preferred_element_type=F32)
        o_ref[:, cols] = (y * ps_ref[:, cols]).astype(o_ref.dtype)


def _pool(u, w_pool, pool_scale, layer):
    n_seq, seq, width = u.shape
    group_w = width // len(POOL_WINDOWS)
    return pl.pallas_call(
        functools.partial(_pool_kernel, seq=seq, group_w=group_w), name='pool',
        grid=(n_seq,),
        in_specs=[pl.BlockSpec((None, seq, width), lambda b: (b, 0, 0)),
                  pl.BlockSpec((None,) + w_pool.shape[1:], lambda b: (layer, 0, 0, 0)),
                  pl.BlockSpec((None, 1, width), lambda b: (layer, 0, 0))],
        out_specs=pl.BlockSpec((None, seq, width), lambda b: (b, 0, 0)),
        out_shape=jax.ShapeDtypeStruct((n_seq, seq, width), BF16),
        compiler_params=_params(("arbitrary",), 5 * _nbytes((seq, width), F32)),
    )(u, w_pool, pool_scale.reshape(pool_scale.shape[0], 1, width))


def _split3(x):
    hi = x.astype(BF16)
    r = x - hi.astype(F32)
    mid = r.astype(BF16)
    lo = (r - mid.astype(F32)).astype(BF16)
    return hi, mid, lo


def _row_select(stack_ref, n, rows):
    row = lax.broadcasted_iota(jnp.int32, stack_ref.shape[1:], 0)
    out = jnp.zeros(stack_ref.shape[1:], F32)
    for h in range(n):
        out = jnp.where(row == h, stack_ref[h], out)
    return out


def _decode_kernel(pt_ref, qf_ref, kfn_ref, vfn_ref, lfn_ref, qd_ref, kdn_ref, vdn_ref, lq_ref, g_ref, *rest,
                   n_slots, n_fox, n_diff, head_dim, scale, lam_init):
    del pt_ref
    pages = [rest[5 * k:5 * k + 5] for k in range(n_slots)]
    (of_ref, od_ref, qbf_ref, qbd_ref, mf_ref, lf_ref, accf_ref, cf_ref, md_ref, ld_ref,
     accd_ref) = rest[5 * n_slots:]
    p = pl.program_id(1)
    rows = qf_ref.shape[0]
    row_id = lax.broadcasted_iota(jnp.int32, (rows, head_dim), 0)

    @pl.when(p == 0)
    def _init():
        qf = qf_ref[...]
        qd = qd_ref[...]
        for h in range(n_fox):
            qbf_ref[h] = jnp.where(row_id == h, qf, 0.0).astype(BF16)
            accf_ref[h] = jnp.broadcast_to(vfn_ref[h:h + 1, :], (rows, head_dim))
        for r in range(2 * n_diff):
            qbd_ref[r] = jnp.where(row_id == r, qd, 0.0).astype(BF16)
        for h in range(n_diff):
            accd_ref[h] = jnp.broadcast_to(vdn_ref[h:h + 1, :], (rows, 2 * head_dim))
        mf_ref[...] = jnp.sum(qf.astype(BF16).astype(F32) * kfn_ref[...].astype(BF16).astype(F32),
                              axis=1, keepdims=True) * scale
        md_ref[...] = jnp.sum(qd.astype(BF16).astype(F32) * kdn_ref[...].astype(BF16).astype(F32),
                              axis=1, keepdims=True) * scale
        lf_ref[...] = jnp.ones_like(lf_ref)
        ld_ref[...] = jnp.ones_like(ld_ref)
        cf_ref[...] = lfn_ref[...]

    page = pages[0][2].shape[1]
    later = (lax.broadcasted_iota(jnp.int32, (page, page), 0)
             > lax.broadcasted_iota(jnp.int32, (page, page), 1)).astype(BF16)

    def softmax_step(s, m_ref, l_ref):
        m_new = jnp.maximum(m_ref[...], jnp.max(s, axis=1, keepdims=True))
        alpha = jnp.exp(m_ref[...] - m_new)
        prob = jnp.exp(s - m_new)
        l_ref[...] = alpha * l_ref[...] + jnp.sum(prob, axis=1, keepdims=True)
        m_ref[...] = m_new
        return alpha, prob.astype(BF16)

    def scores(q_stack_ref, k_ref, n):
        s = _nt_dot(q_stack_ref[0], k_ref[0].astype(BF16))
        for r in range(1, n):
            s = s + _nt_dot(q_stack_ref[r], k_ref[r].astype(BF16))
        return s

    def weighted_values(prob, v_refs, h):
        return sum(jnp.dot(prob[:, k * page:(k + 1) * page], v_ref[h].astype(BF16), preferred_element_type=F32)
                   for k, v_ref in enumerate(v_refs))

    decay = cf_ref[...]
    s_fox, s_diff = [], []
    for ck_ref, _, clf_ref, cdk_ref, _ in pages:
        logf_t = clf_ref[...]
        suffix = sum(jnp.dot(t, later, preferred_element_type=F32) for t in _split3(logf_t))
        s_fox.append(scores(qbf_ref, ck_ref, n_fox) * scale + (decay + suffix))
        s_diff.append(scores(qbd_ref, cdk_ref, 2 * n_diff) * scale)
        decay = decay + jnp.sum(logf_t, axis=1, keepdims=True)
    cf_ref[...] = decay

    alpha, prob = softmax_step(jnp.concatenate(s_fox, axis=1), mf_ref, lf_ref)
    for h in range(n_fox):
        accf_ref[h] = alpha * accf_ref[h] + weighted_values(prob, [pg[1] for pg in pages], h)
    alpha, prob = softmax_step(jnp.concatenate(s_diff, axis=1), md_ref, ld_ref)
    for h in range(n_diff):
        accd_ref[h] = alpha * accd_ref[h] + weighted_values(prob, [pg[4] for pg in pages], h)

    @pl.when(p == pl.num_programs(1) - 1)
    def _finish():
        of_ref[...] = _row_select(accf_ref, n_fox, rows) / lf_ref[...]
        lam = _diff_lambda(lq_ref[...], lam_init)
        od_ref[...] = jnp.zeros_like(od_ref)
        for h in range(n_diff):
            a0 = accd_ref[h][2 * h:2 * h + 1, :] / ld_ref[2 * h:2 * h + 1, :]
            a1 = accd_ref[h][2 * h + 1:2 * h + 2, :] / ld_ref[2 * h + 1:2 * h + 2, :]
            od_ref[h:h + 1, :] = _sub_norm(a0 - lam * a1, g_ref[...], lam_init)


def _decode_attention(page_table, qf, kfn, vfn, lfn, qd, kdn, vdn, lambda_qk, subln_g,
                      cache_fox_k, cache_fox_v, logf_t, cache_diff_k, cache_diff_v, layer,
                      *, n_fox, n_diff, head_dim, lam_init):
    n_seq, n_pages = page_table.shape
    rows = qf.shape[1]
    page = cache_fox_k.shape[3]
    width = 2 * head_dim
    vrows = vdn.shape[1]
    n_slots = _pick(n_pages, (4, 2, 1))

    def per_seq(shape):
        return pl.BlockSpec((None,) + shape, lambda b, p, pt: (b, 0, 0))

    def paged(shape, slot):
        return pl.BlockSpec(
            (None, None) + shape,
            lambda b, p, pt: (layer, pt[b, n_pages - 1 - (p * n_slots + slot)]) + (0,) * len(shape))

    in_specs = [per_seq((rows, head_dim)), per_seq((rows, head_dim)), per_seq((rows, head_dim)),
                per_seq((rows, 1)),
                per_seq((rows, head_dim)), per_seq((rows, head_dim)), per_seq((vrows, width)),
                pl.BlockSpec((None, 4, head_dim), lambda b, p, pt: (layer, 0, 0)),
                pl.BlockSpec((None, 1, width), lambda b, p, pt: (layer, 0, 0))]
    caches = []
    for slot in range(n_slots):
        in_specs += [paged((n_fox, page, head_dim), slot), paged((n_fox, page, head_dim), slot),
                     paged((rows, page), slot),
                     paged((2 * n_diff, page, head_dim), slot), paged((n_diff, page, width), slot)]
        caches += [cache_fox_k, cache_fox_v, logf_t, cache_diff_k, cache_diff_v]
    scratch = [pltpu.VMEM((n_fox, rows, head_dim), BF16), pltpu.VMEM((2 * n_diff, rows, head_dim), BF16),
               pltpu.VMEM((rows, 1), F32), pltpu.VMEM((rows, 1), F32),
               pltpu.VMEM((n_fox, rows, head_dim), F32), pltpu.VMEM((rows, 1), F32),
               pltpu.VMEM((rows, 1), F32), pltpu.VMEM((rows, 1), F32),
               pltpu.VMEM((n_diff, rows, width), F32)]
    vmem = 2 * n_slots * (4 * _nbytes((n_fox, page, head_dim), F32) + _nbytes((rows, page), F32)) + (4 << 20)
    grid_spec = pltpu.PrefetchScalarGridSpec(
        num_scalar_prefetch=1, grid=(n_seq, n_pages // n_slots), in_specs=in_specs,
        out_specs=[pl.BlockSpec((None, rows, head_dim), lambda b, p, pt: (b, 0, 0)),
                   pl.BlockSpec((None, vrows, width), lambda b, p, pt: (b, 0, 0))],
        scratch_shapes=scratch)
    return pl.pallas_call(
        functools.partial(_decode_kernel, n_slots=n_slots, n_fox=n_fox, n_diff=n_diff, head_dim=head_dim,
                          scale=head_dim ** -0.5, lam_init=lam_init),
        grid_spec=grid_spec, name='decode_attn',
        out_shape=[jax.ShapeDtypeStruct((n_seq, rows, head_dim), F32),
                   jax.ShapeDtypeStruct((n_seq, vrows, width), F32)],
        compiler_params=_params(("arbitrary", "arbitrary"), vmem),
    )(page_table, qf, kfn, vfn, lfn, qd, kdn, vdn, lambda_qk,
      subln_g.reshape(subln_g.shape[0], 1, width), *caches)


def _rope_tables(pos, head_dim):
    half = head_dim // 2
    inv_freq = ROPE_THETA ** (-jnp.arange(half, dtype=F32) / half)
    ang = pos.astype(F32)[:, None] * inv_freq[None, :]
    cos, sin = jnp.cos(ang), jnp.sin(ang)
    return jnp.concatenate([cos, cos], axis=-1), jnp.concatenate([-sin, sin], axis=-1)


class _Tiling:
    def __init__(self, n_rows, n_groups):
        self.m = n_rows
        self.tm = _pick(n_rows, (1024, 512, 256, 128))
        self.tm_wide_k = _pick(n_rows, (256, 128))
        self.tm_norm = _pick(n_rows, (256, 128))
        self.n_groups = n_groups

    def tiles_per_group(self, tm):
        return max(1, self.m // self.n_groups // tm)


def _layer(st, x, x_t, mods, mods_t, w, layer, rope, rope_t, dims, attention, attention_t, kv_bufs):
    d = x.shape[1]
    tm, tpg = st.tm, st.tiles_per_group(st.tm)
    tmn, tpgn = st.tm_norm, st.tiles_per_group(st.tm_norm)
    rows_t = x_t.shape[0]
    d_ff = w['w_ffn_out'].shape[2]
    n_fox, n_diff, head_dim, pool_w = dims['n_fox'], dims['n_diff'], dims['head_dim'], dims['pool_w']
    fox_w = n_fox * head_dim

    def pre_norm(x, x_t, idx):
        g = w['norm_g'][layer, idx]
        h = _norm(x, g, mods[3 * idx + 1], mods[3 * idx], tm=tmn, tiles_per_group=tpgn, out_dtype=BF16)
        h_t = _norm(x_t, g, mods_t[3 * idx + 1], mods_t[3 * idx], tm=rows_t, tiles_per_group=1, out_dtype=BF16)
        return h, h_t

    def swiglu_ffn(x, x_t, which, norm_idx):
        h, h_t = pre_norm(x, x_t, norm_idx)
        tn = _pick(d_ff, (256, 128))
        lead = (layer, which)
        (hid,), (hid_t,) = _matmul(
            [h], [(0, w['w_ffn_in'], lead, 0), (0, w['w_ffn_in'], lead, d_ff // tn)], [],
            _ep_swiglu, [BF16], tm=tm, tn=tn, n_tiles=d_ff // tn, name='ffn_in', tail=([h_t], []))
        tm2 = st.tm_wide_k
        tn2 = _pick(d, (512, 256, 128))
        gate, gate_t = mods[3 * norm_idx + 2], mods_t[3 * norm_idx + 2]
        (x_new,), (x_new_t,) = _matmul(
            [hid], [(0, w['w_ffn_out'], lead, 0)],
            [_tile_extra(x, tm2, tn2), _mod_extra(gate, tn2, st.tiles_per_group(tm2))],
            functools.partial(_ep_residual, 0.5), [F32], tm=tm2, tn=tn2, n_tiles=d // tn2,
            name='ffn_out', single_buffer_weights=True,
            tail=([hid_t], [_tail_tile(x_t, tn2), _tail_mod(gate_t, tn2)]))
        return x_new, x_new_t

    x, x_t = swiglu_ffn(x, x_t, 0, 0)

    h, h_t = pre_norm(x, x_t, 1)
    tn = _pick(fox_w, (512, 256, 128))
    nt = fox_w // tn
    w_in = w['w_in']

    def proj(weight, lead, col0, n_tiles, out_dtype, rope_it=False, buf=None):
        if rope_it:
            tiles = rope[0].shape[0] // tm
            extras = [(t, (tm, head_dim), lambda j, i: (i % tiles, 0)) for t in rope]
            extras_t = [(t, (rows_t, head_dim), lambda j, i: (0, 0)) for t in rope_t]
            ep = functools.partial(_ep_rope, head_dim)
        else:
            extras, extras_t, ep = [], [], _ep_plain
        (out,), (out_t,) = _matmul(
            [h], [(0, weight, lead, col0)], extras, ep, [out_dtype], tm=tm, tn=tn, n_tiles=n_tiles,
            name='proj_rope' if rope_it else 'proj', head_major=None if buf is None else (buf, layer, tpg),
            tail=([h_t], extras_t))
        return out, out_t

    fk_buf, fv_buf, dk_buf, dv_buf = kv_bufs
    fq, fq_t = proj(w_in, (layer,), 0, nt, BF16)
    fk, fk_t = proj(w_in, (layer,), nt, nt, F32, buf=fk_buf)
    fv, fv_t = proj(w_in, (layer,), 2 * nt, nt, F32, buf=fv_buf)
    w_d = w['w_in_diff']
    dq, dq_t = proj(w_d, (layer,), 0, nt, BF16, rope_it=True)
    dk, dk_t = proj(w_d, (layer,), nt, nt, F32, rope_it=True, buf=dk_buf)
    dv, dv_t = proj(w_d, (layer,), 2 * nt, nt, F32, buf=dv_buf)
    u, u_t = proj(w_d, (layer,), 3 * nt, pool_w // tn, F32)
    (z,), (z_t,) = _matmul([h], [(0, w['w_in_forget'], (layer,), 0)], [], _ep_plain, [F32],
                           tm=tm, tn=LANES_V7X, n_tiles=1, name='proj_forget', tail=([h_t], []))

    y_fox, y_diff, y_pool, logf, u_state = attention(fq, fk, fv, z, dq, dk, dv, u)
    y_fox_t, y_diff_t, y_pool_t, logf_t, u_state_t = attention_t(fq_t, fk_t, fv_t, z_t, dq_t, dk_t, dv_t, u_t)

    tng = _pick(d, (512, 256, 128))
    ntg = d // tng
    bias = (w['b_gate'].reshape(-1, 1, 3 * d), (None, 1, tng), lambda j, i: (layer, 0, j))
    (gates,), (gates_t,) = _matmul([h], [(0, w['w_gate'], (layer,), 0)], [bias],
                                   _ep_sigmoid_bias, [BF16], tm=tm, tn=tng, n_tiles=3 * ntg, name='gates',
                                   tail=([h_t], [bias]))
    (merged,), (merged_t,) = _matmul(
        [y_fox, y_diff, y_pool],
        [(0, w['w_br_fox'], (layer,), 0), (1, w['w_br_diff'], (layer,), 0), (2, w['w_br_pool'], (layer,), 0)],
        [_tile_extra(gates, tm, tng, b * ntg) for b in range(3)],
        _ep_merge, [BF16], tm=tm, tn=tng, n_tiles=ntg, name='merge',
        tail=([y_fox_t, y_diff_t, y_pool_t], [_tail_tile(gates_t, tng, b * ntg) for b in range(3)]))
    (x,), (x_t,) = _matmul([merged], [(0, w['w_o'], (layer,), 0)],
                           [_tile_extra(x, tm, tng), _mod_extra(mods[5], tng, tpg)],
                           functools.partial(_ep_residual, 1.0), [F32], tm=tm, tn=tng, n_tiles=ntg, name='w_o',
                           tail=([merged_t], [_tail_tile(x_t, tng), _tail_mod(mods_t[5], tng)]))

    x, x_t = swiglu_ffn(x, x_t, 1, 2)
    return x, x_t, (fk, fv, logf, dk, dv, u_state), (fk_t, fv_t, logf_t, dk_t, dv_t, u_state_t)


def kernel(x_prompt, x_sample, c_prompt, c_sample, page_table, cache_fox_k, cache_fox_v, cache_fox_logf, cache_diff_k, cache_diff_v, state_pool, norm_g, w_ada, b_ada, w_ffn_in, w_ffn_out, w_in, b_forget, lambda_qk, subln_g, w_pool, pool_scale, w_br_fox, w_br_diff, w_br_pool, w_gate, b_gate, w_o, final_norm_g):
    n_seq, seq, d = x_prompt.shape
    n_dec, dec_seq, _ = x_sample.shape
    assert dec_seq == 1 and n_dec <= SAMPLE_ROWS
    depth = norm_g.shape[0]
    n_fox, head_dim = cache_fox_k.shape[3], cache_fox_k.shape[4]
    n_diff = cache_diff_v.shape[3]
    fox_w = n_fox * head_dim
    pool_w = w_pool.shape[1] * w_pool.shape[2]
    pool_state = state_pool.shape[2]
    n_pages, page = page_table.shape[1], cache_fox_k.shape[2]
    past_len = n_pages * page
    dims = dict(n_fox=n_fox, n_diff=n_diff, head_dim=head_dim, pool_w=pool_w)
    rows = SAMPLE_ROWS
    assert n_fox <= rows and 2 * n_diff <= rows and n_fox <= LANES_V7X

    off_ff = 3 * fox_w
    off_d = off_ff + n_fox
    weights = dict(norm_g=norm_g, w_ffn_in=w_ffn_in, w_ffn_out=w_ffn_out, w_in=w_in[:, :, :off_ff], w_gate=w_gate,
                   b_gate=b_gate, w_br_fox=w_br_fox, w_br_diff=w_br_diff, w_br_pool=w_br_pool, w_o=w_o,
                   w_in_diff=w_in[:, :, off_d:],
                   w_in_forget=jnp.pad(w_in[:, :, off_ff:off_d], ((0, 0), (0, 0), (0, LANES_V7X - n_fox))))
    b_forget_pad = jnp.pad(b_forget, ((0, 0), (0, LANES_V7X - n_fox)))

    n_c = n_seq + n_dec
    c_rows = -(-n_c // BF16_ROWS_V7X) * BF16_ROWS_V7X
    c_all = jnp.pad(jnp.concatenate([c_prompt, c_sample], axis=0), ((0, c_rows - n_c), (0, 0)))
    tn_ada = _pick(N_MOD * d, (1024, 512, 256, 128))
    mods_p, mods_s = [], []
    for l in range(depth):
        (mod,) = _matmul([c_all], [(0, w_ada, (l,), 0)],
                         [(b_ada.reshape(depth, 1, N_MOD * d), (None, 1, tn_ada), lambda j, i, l=l: (l, 0, j))],
                         _ep_bias, [F32], tm=c_rows, tn=tn_ada, n_tiles=N_MOD * d // tn_ada, name='ada',
                         prologue=_silu_bf16)
        mod = mod.reshape(c_rows, N_MOD, d)
        mods_p.append([mod[:n_seq, k][:, None, :] for k in range(N_MOD)])
        mods_s.append([jnp.pad(mod[n_seq:n_c, k], ((0, rows - n_dec), (0, 0)))[None] for k in range(N_MOD)])

    st_p = _Tiling(n_seq * seq, n_seq)
    rope_p = _rope_tables(jnp.arange(seq, dtype=jnp.int32), head_dim)
    rope_s = tuple(jnp.broadcast_to(t, (rows, head_dim))
                   for t in _rope_tables(jnp.full((1,), past_len, jnp.int32), head_dim))
    logf_t = jnp.pad(jnp.swapaxes(cache_fox_logf, 2, 3), ((0, 0), (0, 0), (0, rows - n_fox), (0, 0)))
    head_major = (0, 1, 3, 2, 4)
    cache_views = [jnp.transpose(c, head_major) for c in (cache_fox_k, cache_fox_v, cache_diff_k, cache_diff_v)]
    kv_bufs = [jnp.zeros((depth, n_seq, n_fox, seq, head_dim), F32),
               jnp.zeros((depth, n_seq, n_fox, seq, head_dim), F32),
               jnp.zeros((depth, n_seq, 2 * n_diff, seq, head_dim), F32),
               jnp.zeros((depth, n_seq, n_diff, seq, 2 * head_dim), F32)]

    def attention_prompt(l, fq, fk, fv, z, dq, dk, dv, u):
        logf, cum_cols, cum_rows = _logf(z, b_forget_pad[l:l + 1], n_seq=n_seq, seq=seq, n_heads=n_fox,
                                         cumulative=True)
        y_fox = _fox_attention(fq, fk, fv, l, cum_cols, cum_rows, n_seq=n_seq, seq=seq, n_heads=n_fox,
                               head_dim=head_dim)
        lam_init = 0.8 - 0.6 * math.exp(-0.3 * l)
        y_diff = _diff_attention(dq, dk, dv, lambda_qk, subln_g, l, n_seq=n_seq, seq=seq, n_heads=n_diff,
                                 head_dim=head_dim, lam_init=lam_init)
        u3 = u.reshape(n_seq, seq, pool_w)
        y_pool = _pool(u3, w_pool, pool_scale, l).reshape(n_seq * seq, pool_w)
        return y_fox, y_diff, y_pool, logf, u3[:, seq - pool_state:]

    def attention_sample(l, fq, fk, fv, z, dq, dk, dv, u):
        logf, cum_cols, _ = _logf(z, b_forget_pad[l:l + 1], n_seq=1, seq=rows, n_heads=n_fox, cumulative=False)

        def heads(a, n, width):
            a = a[:n_dec].astype(F32).reshape(n_dec, n, width)
            return jnp.pad(a, ((0, 0), (0, -(-n // SUBLANES_V7X) * SUBLANES_V7X - n), (0, 0)))

        def heads16(a, n):
            a = heads(a, n, head_dim)
            return jnp.pad(a, ((0, 0), (0, rows - a.shape[1]), (0, 0)))

        lfn = jnp.pad(cum_cols[:n_dec, :n_fox], ((0, 0), (0, rows - n_fox)))[:, :, None]
        lam_init = 0.8 - 0.6 * math.exp(-0.3 * l)
        o_fox, o_diff = _decode_attention(
            page_table, heads16(fq, n_fox), heads16(fk, n_fox), heads16(fv, n_fox), lfn,
            heads16(dq, 2 * n_diff), heads16(dk, 2 * n_diff), heads(dv, n_diff, 2 * head_dim),
            lambda_qk, subln_g, cache_views[0], cache_views[1], logf_t, cache_views[2], cache_views[3], l,
            n_fox=n_fox, n_diff=n_diff, head_dim=head_dim, lam_init=lam_init)

        def unheads(o, n):
            o = o[:, :n].reshape(n_dec, -1)
            return jnp.pad(o, ((0, rows - n_dec), (0, 0))).astype(BF16)

        u_ext = jnp.concatenate([state_pool[l], u[:n_dec, None, :]], axis=1)
        y_pool = _pool(u_ext, w_pool, pool_scale, l)[:, -1]
        y_pool = jnp.pad(y_pool, ((0, rows - n_dec), (0, 0)))
        return unheads(o_fox, n_fox), unheads(o_diff, n_diff), y_pool, logf, u_ext[:, 1:]

    xp = x_prompt.reshape(n_seq * seq, d)
    xs = jnp.pad(x_sample.reshape(n_dec, d), ((0, rows - n_dec), (0, 0)))
    states_p, states_s = [], []
    for l in range(depth):
        xp, xs, sp, ss = _layer(st_p, xp, xs, mods_p[l], mods_s[l], weights, l, rope_p, rope_s, dims,
                                functools.partial(attention_prompt, l), functools.partial(attention_sample, l),
                                kv_bufs)
        kv_bufs = [sp[0], sp[1], sp[3], sp[4]]
        states_p.append(sp)
        states_s.append(ss)

    y_prompt = _norm(xp, final_norm_g, None, None, tm=st_p.tm_norm, tiles_per_group=1, out_dtype=F32)
    y_sample = _norm(xs, final_norm_g, None, None, tm=rows, tiles_per_group=1, out_dtype=F32)

    def stack_p(i, shape):
        return jnp.stack([s[i].reshape(shape) for s in states_p], axis=0)

    def stack_s(i, shape):
        return jnp.stack([s[i][:n_dec].reshape(shape) for s in states_s], axis=0)

    return (y_prompt.reshape(n_seq, seq, d), y_sample[:n_dec].reshape(n_dec, 1, d),
            jnp.transpose(kv_bufs[0], head_major), jnp.transpose(kv_bufs[1], head_major),
            stack_p(2, (n_seq, seq, n_fox)),
            jnp.transpose(kv_bufs[2], head_major), jnp.transpose(kv_bufs[3], head_major),
            jnp.stack([s[5] for s in states_p], axis=0),
            stack_s(0, (n_dec, 1, n_fox, head_dim)), stack_s(1, (n_dec, 1, n_fox, head_dim)),
            stack_s(2, (n_dec, 1, n_fox)),
            stack_s(3, (n_dec, 1, 2 * n_diff, head_dim)), stack_s(4, (n_dec, 1, n_diff, 2 * head_dim)),
            jnp.stack([s[5] for s in states_s], axis=0))
```

```python
import functools
import math

import jax
import jax.numpy as jnp
from jax import lax
from jax.experimental import pallas as pl
from jax.experimental.pallas import tpu as pltpu

F32 = jnp.float32
BF16 = jnp.bfloat16

LANES_V7X = 128
SUBLANES_V7X = 8
BF16_ROWS_V7X = 16
VMEM_CAP_V7X = 58 * 1024 * 1024

POOL_WINDOWS = (2, 4, 8, 16)
ROPE_THETA = 10000.0
EPS = 1e-6
N_MOD = 9
SAMPLE_ROWS = BF16_ROWS_V7X
X_RING_SLOTS = 3


def _pick(n, prefs):
    for p in prefs:
        if n % p == 0:
            return p
    return n


def _nbytes(shape, dtype):
    return math.prod(shape) * jnp.dtype(dtype).itemsize


def _params(sem, vmem_estimate):
    limit = min(VMEM_CAP_V7X, int(vmem_estimate * 5 // 4) + (4 << 20))
    return pltpu.CompilerParams(dimension_semantics=sem, vmem_limit_bytes=limit)


def _w_imap(lead, col0, j, i):
    return (*lead, 0, col0 + j)


def _x_imap(j, i):
    return (i, 0)


def _out_imap(j, i):
    return (i, j)


def _mm_kernel(*refs, n_x, n_p, n_e, n_o, n_alias, with_tail, x_of, prologue, epilogue, head_w, ring_tm,
               w_tiles):
    x_refs, refs = refs[:n_x], refs[n_x:]
    w_refs, refs = refs[:n_p], refs[n_p:]
    e_refs, refs = refs[:n_e], refs[n_e + n_alias:]
    if with_tail:
        xt_refs, refs = refs[:n_x], refs[n_x:]
        et_refs, refs = refs[:n_e], refs[n_e:]
    o_refs, refs = refs[:n_o], refs[n_o:]
    if with_tail:
        ot_refs, refs = refs[:n_o], refs[n_o:]
    if ring_tm is not None:
        xbuf_refs, sem_ref, refs = refs[:n_x], refs[n_x], refs[n_x + 1:]
        n_i = pl.num_programs(1)
        n_steps = pl.num_programs(0) * n_i
        step = pl.program_id(0) * n_i + pl.program_id(1)

        def tile_copy(k, at_step):
            row = pl.multiple_of((at_step % n_i) * ring_tm, ring_tm)
            slot = at_step % X_RING_SLOTS
            return pltpu.make_async_copy(x_refs[k].at[pl.ds(row, ring_tm), :], xbuf_refs[k].at[slot],
                                         sem_ref.at[k, slot])

        @pl.when(step == 0)
        def _prime():
            for ahead in range(X_RING_SLOTS - 1):
                for k in range(n_x):
                    tile_copy(k, ahead).start()

        @pl.when(step + X_RING_SLOTS - 1 < n_steps)
        def _prefetch():
            for k in range(n_x):
                tile_copy(k, step + X_RING_SLOTS - 1).start()

        for k in range(n_x):
            tile_copy(k, step).wait()
        x_refs = [xbuf_refs[k].at[step % X_RING_SLOTS] for k in range(n_x)]
    if w_tiles is not None:
        wf_refs, wsem_ref, refs = refs[:n_p], refs[n_p], refs[n_p + 1:]
        col = pl.program_id(0)
        tn = wf_refs[0].shape[2]

        def weight_copy(p, at_col):
            lead, col0 = w_tiles[p]
            start = pl.multiple_of((col0 + at_col) * tn, tn)
            return pltpu.make_async_copy(w_refs[p].at[(*lead, slice(None), pl.ds(start, tn))],
                                         wf_refs[p].at[at_col % 2], wsem_ref.at[p, at_col % 2])
    wb_refs = refs

    def compute(x_refs, e_refs):
        xs = [x_ref[...] for x_ref in x_refs]
        if prologue is not None:
            xs = [prologue(x) for x in xs]
        accs = [jnp.dot(xs[x_of[p]], wb_refs[p][...], preferred_element_type=F32) for p in range(n_p)]
        return epilogue(accs, [e_ref[...] for e_ref in e_refs])

    @pl.when(pl.program_id(1) == 0)
    def _new_column_tile():
        if w_tiles is None:
            for w_ref, wb_ref in zip(w_refs, wb_refs):
                wb_ref[...] = w_ref[...].astype(BF16)
        else:
            @pl.when(col == 0)
            def _first_tile():
                for p in range(n_p):
                    weight_copy(p, 0).start()

            @pl.when(col + 1 < pl.num_programs(0))
            def _next_tile():
                for p in range(n_p):
                    weight_copy(p, col + 1).start()

            for p in range(n_p):
                weight_copy(p, col).wait()
                wb_refs[p][...] = wf_refs[p][col % 2].astype(BF16)
        if with_tail:
            for o_ref, o in zip(ot_refs, compute(xt_refs, et_refs)):
                o_ref[...] = o.astype(o_ref.dtype)

    for o_ref, o in zip(o_refs, compute(x_refs, e_refs)):
        if head_w is None:
            o_ref[...] = o.astype(o_ref.dtype)
        else:
            for h in range(o_ref.shape[0]):
                o_ref[h] = o[:, h * head_w:(h + 1) * head_w].astype(o_ref.dtype)


def _matmul(xs, products, extras, epilogue, out_dtypes, *, tm, tn, n_tiles, name, prologue=None, head_major=None,
            single_buffer_weights=False, tail=None):
    m = xs[0].shape[0]
    grid = (n_tiles, m // tm)
    in_specs, vmem = [], 0
    ring = m // tm > 1
    scratch = []
    for x in xs:
        if ring:
            in_specs.append(pl.BlockSpec(memory_space=pl.ANY))
            scratch.append(pltpu.VMEM((X_RING_SLOTS, tm, x.shape[1]), x.dtype))
        else:
            in_specs.append(pl.BlockSpec((tm, x.shape[1]), _x_imap))
        vmem += (X_RING_SLOTS if ring else 2) * _nbytes((tm, x.shape[1]), x.dtype)
    if ring:
        scratch.append(pltpu.SemaphoreType.DMA((len(xs), X_RING_SLOTS)))
    w_ring = ring and not single_buffer_weights
    wb_scratch = []
    for _, w, lead, col0 in products:
        k = w.shape[-2]
        mode = dict(pipeline_mode=pl.Buffered(1)) if single_buffer_weights else {}
        if w_ring:
            in_specs.append(pl.BlockSpec(memory_space=pl.ANY))
            scratch.append(pltpu.VMEM((2, k, tn), w.dtype))
        else:
            in_specs.append(pl.BlockSpec((None,) * len(lead) + (k, tn), functools.partial(_w_imap, lead, col0),
                                         **mode))
        wb_scratch.append(pltpu.VMEM((k, tn), BF16))
        vmem += (1 if single_buffer_weights else 2) * _nbytes((k, tn), w.dtype) + _nbytes((k, tn), BF16)
    if w_ring:
        scratch.append(pltpu.SemaphoreType.DMA((len(products), 2)))
    scratch += wb_scratch
    for arr, block, imap in extras:
        in_specs.append(pl.BlockSpec(block, imap))
        vmem += 2 * _nbytes([b for b in block if b is not None], arr.dtype)
    operands = [*xs, *[p[1] for p in products], *[e[0] for e in extras]]
    aliases, head_w = {}, None
    if head_major is None:
        out_shape = [jax.ShapeDtypeStruct((m, n_tiles * tn), dt) for dt in out_dtypes]
        out_specs = [pl.BlockSpec((tm, tn), _out_imap) for _ in out_dtypes]
    else:
        buf, layer, seq_tiles = head_major
        head_w = buf.shape[-1]
        out_shape = [jax.ShapeDtypeStruct(buf.shape, buf.dtype)]
        out_specs = [pl.BlockSpec((None, None, tn // head_w, tm, head_w),
                                  lambda j, i: (layer, i // seq_tiles, j, i % seq_tiles, 0))]
        in_specs.append(pl.BlockSpec(memory_space=pl.ANY))
        aliases = {len(operands): 0}
        operands.append(buf)
    if tail is not None:
        xs_tail, extras_tail = tail
        rows = xs_tail[0].shape[0]
        for x in xs_tail:
            in_specs.append(pl.BlockSpec((rows, x.shape[1]), lambda j, i: (0, 0)))
            vmem += 2 * _nbytes((rows, x.shape[1]), x.dtype)
        for arr, block, imap in extras_tail:
            in_specs.append(pl.BlockSpec(block, imap))
            vmem += 2 * _nbytes([b for b in block if b is not None], arr.dtype)
        operands += [*xs_tail, *[e[0] for e in extras_tail]]
        out_shape = out_shape + [jax.ShapeDtypeStruct((rows, n_tiles * tn), dt) for dt in out_dtypes]
        out_specs = out_specs + [pl.BlockSpec((rows, tn), lambda j, i: (0, j)) for _ in out_dtypes]
    vmem += sum(2 * _nbytes((tm, tn), dt) for dt in out_dtypes) + 4 * len(products) * _nbytes((tm, tn), F32)
    body = functools.partial(
        _mm_kernel, n_x=len(xs), n_p=len(products), n_e=len(extras), n_o=len(out_dtypes),
        n_alias=len(aliases), with_tail=tail is not None, x_of=tuple(p[0] for p in products),
        prologue=prologue, epilogue=epilogue, head_w=head_w, ring_tm=tm if ring else None,
        w_tiles=tuple((p[2], p[3]) for p in products) if w_ring else None)
    outs = pl.pallas_call(
        body, grid=grid, in_specs=in_specs, out_specs=out_specs, out_shape=out_shape,
        scratch_shapes=scratch, name=name, input_output_aliases=aliases,
        compiler_params=_params(("arbitrary", "arbitrary"), vmem),
    )(*operands)
    return outs if tail is None else (outs[:len(out_dtypes)], outs[len(out_dtypes):])


def _mod_extra(arr, tn, tiles_per_group):
    _, r, _ = arr.shape
    return (arr, (None, r, tn), lambda j, i: (i // tiles_per_group, 0, j))


def _tile_extra(arr, tm, tn, col0=0):
    return (arr, (tm, tn), lambda j, i: (i, col0 + j))


def _tail_mod(arr, tn):
    return (arr, (None, arr.shape[1], tn), lambda j, i: (0, 0, j))


def _tail_tile(arr, tn, col0=0):
    return (arr, (arr.shape[0], tn), lambda j, i: (0, col0 + j))


def _ep_plain(accs, extras):
    return [accs[0]]


def _ep_bias(accs, extras):
    return [accs[0] + extras[0]]


def _ep_swiglu(accs, extras):
    gate, up = accs
    return [gate * jax.nn.sigmoid(gate) * up]


def _ep_sigmoid_bias(accs, extras):
    return [jax.nn.sigmoid(accs[0] + extras[0])]


def _ep_residual(coef, accs, extras):
    x_res, gate = extras
    return [x_res + (coef * gate) * accs[0]]


def _ep_merge(accs, extras):
    out = extras[0].astype(F32) * accs[0]
    for g, a in zip(extras[1:], accs[1:]):
        out = out + g.astype(F32) * a
    return [out]


def _ep_rope(head_dim, accs, extras):
    cos, sin_signed = extras
    acc = accs[0]
    heads = []
    for h in range(acc.shape[1] // head_dim):
        a = acc[:, h * head_dim:(h + 1) * head_dim]
        heads.append(a * cos + pltpu.roll(a, head_dim // 2, 1) * sin_signed)
    return [jnp.concatenate(heads, axis=1) if len(heads) > 1 else heads[0]]


def _silu_bf16(x):
    return (x * jax.nn.sigmoid(x)).astype(BF16)


def _norm_kernel(x_ref, g_ref, *rest, modulated):
    x = x_ref[...]
    y = x * lax.rsqrt(jnp.mean(x * x, axis=-1, keepdims=True) + EPS) * g_ref[...]
    if modulated:
        scale_ref, shift_ref, o_ref = rest
        y = y * (1.0 + scale_ref[...]) + shift_ref[...]
    else:
        (o_ref,) = rest
    o_ref[...] = y.astype(o_ref.dtype)


def _norm(x, g, scale, shift, *, tm, tiles_per_group, out_dtype):
    m, d = x.shape
    in_specs = [pl.BlockSpec((tm, d), lambda i: (i, 0)), pl.BlockSpec((1, d), lambda i: (0, 0))]
    args = [x, g.reshape(1, d)]
    if scale is not None:
        r = scale.shape[1]
        spec = pl.BlockSpec((None, r, d), lambda i: (i // tiles_per_group, 0, 0))
        in_specs += [spec, spec]
        args += [scale, shift]
    vmem = 6 * _nbytes((tm, d), F32)
    return pl.pallas_call(
        functools.partial(_norm_kernel, modulated=scale is not None), name='norm',
        grid=(m // tm,), in_specs=in_specs, out_specs=pl.BlockSpec((tm, d), lambda i: (i, 0)),
        out_shape=jax.ShapeDtypeStruct((m, d), out_dtype),
        compiler_params=_params(("arbitrary",), vmem),
    )(*args)


def _shift_rows(x, k):
    row = lax.broadcasted_iota(jnp.int32, x.shape, 0)
    return jnp.where(row >= k, pltpu.roll(x, k, 0), 0.0)


def _logf_kernel(z_ref, b_ref, logf_ref, cum_ref, cum_t_ref, *, n_heads, seq, cumulative):
    z = z_ref[...] + b_ref[...]
    logf = jnp.minimum(z, 0.0) - jnp.log1p(jnp.exp(-jnp.abs(z)))
    logf_ref[...] = logf[:, :n_heads]
    cum = logf
    if cumulative:
        k = 1
        while k < seq:
            cum = cum + _shift_rows(cum, k)
            k *= 2
    cum_ref[...] = cum
    cum_t_ref[...] = jnp.transpose(cum)[:cum_t_ref.shape[0], :]


def _logf(z, b_pad, *, n_seq, seq, n_heads, cumulative):
    m, lanes = z.shape
    head_rows = BF16_ROWS_V7X
    return pl.pallas_call(
        functools.partial(_logf_kernel, n_heads=n_heads, seq=seq, cumulative=cumulative), name='logf',
        grid=(n_seq,),
        in_specs=[pl.BlockSpec((seq, lanes), lambda b: (b, 0)), pl.BlockSpec((1, lanes), lambda b: (0, 0))],
        out_specs=[pl.BlockSpec((seq, n_heads), lambda b: (b, 0)),
                   pl.BlockSpec((seq, lanes), lambda b: (b, 0)),
                   pl.BlockSpec((None, head_rows, seq), lambda b: (b, 0, 0))],
        out_shape=[jax.ShapeDtypeStruct((m, n_heads), F32), jax.ShapeDtypeStruct((m, lanes), F32),
                   jax.ShapeDtypeStruct((n_seq, head_rows, seq), F32)],
        compiler_params=_params(("arbitrary",), 16 * _nbytes((seq, lanes), F32)),
    )(z, b_pad)


def _causal_mask(s):
    row = lax.broadcasted_iota(jnp.int32, s.shape, 0)
    col = lax.broadcasted_iota(jnp.int32, s.shape, 1)
    return jnp.where(col <= row, s, -jnp.inf)


def _softmax_accumulate(s, v, state):
    m_chunk = jnp.max(s, axis=1, keepdims=True)
    if state is None:
        p = jnp.exp(s - m_chunk)
        return m_chunk, jnp.sum(p, axis=1, keepdims=True), jnp.dot(p.astype(BF16), v, preferred_element_type=F32)
    m, denom, acc = state
    m_new = jnp.maximum(m, m_chunk)
    alpha = jnp.exp(m - m_new)
    p = jnp.exp(s - m_new)
    return (m_new, alpha * denom + jnp.sum(p, axis=1, keepdims=True),
            alpha * acc + jnp.dot(p.astype(BF16), v, preferred_element_type=F32))


def _nt_dot(a, b):
    return lax.dot_general(a, b, (((1,), (1,)), ((), ())), preferred_element_type=F32)


def _fox_kernel(q_ref, k_ref, v_ref, cc_ref, cr_ref, o_ref, kb_ref, vb_ref, *, tq, nq, scale):
    h = pl.program_id(1)
    qi = pl.program_id(2)

    @pl.when(qi == 0)
    def _cast_kv():
        kb_ref[...] = k_ref[...].astype(BF16)
        vb_ref[...] = v_ref[...].astype(BF16)

    q = q_ref[...]
    lane = lax.broadcasted_iota(jnp.int32, cc_ref.shape, 1)
    cum_q = jnp.sum(jnp.where(lane == h, cc_ref[...], 0.0), axis=1, keepdims=True)
    for n in range(nq):
        @pl.when(qi == n)
        def _block(n=n):
            state = None
            for c in range(n + 1):
                keys = slice(c * tq, (c + 1) * tq)
                s = _nt_dot(q, kb_ref[keys, :]) * scale + (cum_q - cr_ref[pl.ds(h, 1), keys])
                if c == n:
                    s = _causal_mask(s)
                state = _softmax_accumulate(s, vb_ref[keys, :], state)
            _, denom, acc = state
            o_ref[...] = (acc / denom).astype(o_ref.dtype)


def _fox_attention(q, k_buf, v_buf, layer, cum_cols, cum_rows, *, n_seq, seq, n_heads, head_dim):
    m = q.shape[0]
    tq = _pick(seq, (512, 256, 128))
    nq = seq // tq
    lanes = cum_cols.shape[1]
    vmem = (4 * _nbytes((seq, head_dim), F32) + 2 * _nbytes((seq, head_dim), BF16)
            + 6 * _nbytes((tq, seq), F32) + 4 * _nbytes((tq, lanes), F32))
    kv_spec = pl.BlockSpec((None, None, None, seq, head_dim), lambda b, h, i: (layer, b, h, 0, 0))
    return pl.pallas_call(
        functools.partial(_fox_kernel, tq=tq, nq=nq, scale=head_dim ** -0.5), name='fox_attn',
        grid=(n_seq, n_heads, nq),
        in_specs=[pl.BlockSpec((tq, head_dim), lambda b, h, i: (b * nq + i, h)), kv_spec, kv_spec,
                  pl.BlockSpec((tq, lanes), lambda b, h, i: (b * nq + i, 0)),
                  pl.BlockSpec((None, cum_rows.shape[1], seq), lambda b, h, i: (b, 0, 0))],
        out_specs=pl.BlockSpec((tq, head_dim), lambda b, h, i: (b * nq + i, h)),
        out_shape=jax.ShapeDtypeStruct((m, n_heads * head_dim), BF16),
        scratch_shapes=[pltpu.VMEM((seq, head_dim), BF16), pltpu.VMEM((seq, head_dim), BF16)],
        compiler_params=_params(("arbitrary", "arbitrary", "arbitrary"), vmem),
    )(q, k_buf, v_buf, cum_cols, cum_rows)


def _diff_lambda(lq, lam_init):
    a = jnp.sum(lq[0:1, :] * lq[1:2, :], axis=1, keepdims=True)
    b = jnp.sum(lq[2:3, :] * lq[3:4, :], axis=1, keepdims=True)
    return jnp.exp(a) - jnp.exp(b) + lam_init


def _sub_norm(o, g, lam_init):
    return o * lax.rsqrt(jnp.mean(o * o, axis=-1, keepdims=True) + EPS) * g * (1.0 - lam_init)


def _diff_kernel(q_ref, k_ref, v_ref, lq_ref, g_ref, o_ref, kb_ref, vb_ref, *, tq, nq, scale, head_dim, lam_init):
    qi = pl.program_id(2)

    @pl.when(qi == 0)
    def _cast_kv():
        kb_ref[...] = k_ref[...].astype(BF16)
        vb_ref[...] = v_ref[...].astype(BF16)

    q = q_ref[...]
    lam = _diff_lambda(lq_ref[...], lam_init)
    for n in range(nq):
        @pl.when(qi == n)
        def _block(n=n):
            states = [None, None]
            for c in range(n + 1):
                keys = slice(c * tq, (c + 1) * tq)
                for comp in range(2):
                    cols = slice(comp * head_dim, (comp + 1) * head_dim)
                    s = _nt_dot(q[:, cols], kb_ref[comp, keys, :]) * scale
                    if c == n:
                        s = _causal_mask(s)
                    states[comp] = _softmax_accumulate(s, vb_ref[keys, :], states[comp])
            (_, d0, acc0), (_, d1, acc1) = states
            o = acc0 / d0 - lam * (acc1 / d1)
            o_ref[...] = _sub_norm(o, g_ref[...], lam_init).astype(o_ref.dtype)


def _diff_attention(q, k_buf, v_buf, lambda_qk, subln_g, layer, *, n_seq, seq, n_heads, head_dim, lam_init):
    m = q.shape[0]
    tq = _pick(seq, (512, 256, 128))
    nq = seq // tq
    width = 2 * head_dim
    vmem = (4 * _nbytes((seq, width), F32) + 2 * _nbytes((seq, width), BF16) + 8 * _nbytes((tq, seq), F32))
    return pl.pallas_call(
        functools.partial(_diff_kernel, tq=tq, nq=nq, scale=head_dim ** -0.5, head_dim=head_dim,
                          lam_init=lam_init), name='diff_attn',
        grid=(n_seq, n_heads, nq),
        in_specs=[pl.BlockSpec((tq, width), lambda b, h, i: (b * nq + i, h)),
                  pl.BlockSpec((None, None, 2, seq, head_dim), lambda b, h, i: (layer, b, h, 0, 0)),
                  pl.BlockSpec((None, None, None, seq, width), lambda b, h, i: (layer, b, h, 0, 0)),
                  pl.BlockSpec((None, 4, head_dim), lambda b, h, i: (layer, 0, 0)),
                  pl.BlockSpec((None, 1, width), lambda b, h, i: (layer, 0, 0))],
        out_specs=pl.BlockSpec((tq, width), lambda b, h, i: (b * nq + i, h)),
        out_shape=jax.ShapeDtypeStruct((m, n_heads * width), BF16),
        scratch_shapes=[pltpu.VMEM((2, seq, head_dim), BF16), pltpu.VMEM((seq, width), BF16)],
        compiler_params=_params(("arbitrary", "arbitrary", "arbitrary"), vmem),
    )(q, k_buf, v_buf, lambda_qk, subln_g.reshape(subln_g.shape[0], 1, width))


def _pool_kernel(u_ref, w_ref, ps_ref, o_ref, *, seq, group_w):
    row = lax.broadcasted_iota(jnp.int32, (seq, 1), 0)
    for g, window in enumerate(POOL_WINDOWS):
        cols = slice(g * group_w, (g + 1) * group_w)
        u = u_ref[:, cols]
        total = u
        k = 1
        while k < window:
            total = total + _shift_rows(total, k)
            k *= 2
        count = jnp.minimum(row + 1, window).astype(F32)
        pooled = total / count - u
        y = jnp.dot(pooled.astype(BF16), w_ref[g].astype(BF16), preferred_element_type=F32)
        o_ref[:, cols] = (y * ps_ref[:, cols]).astype(o_ref.dtype)


def _pool(u, w_pool, pool_scale, layer):
    n_seq, seq, width = u.shape
    group_w = width // len(POOL_WINDOWS)
    return pl.pallas_call(
        functools.partial(_pool_kernel, seq=seq, group_w=group_w), name='pool',
        grid=(n_seq,),
        in_specs=[pl.BlockSpec((None, seq, width), lambda b: (b, 0, 0)),
                  pl.BlockSpec((None,) + w_pool.shape[1:], lambda b: (layer, 0, 0, 0)),
                  pl.BlockSpec((None, 1, width), lambda b: (layer, 0, 0))],
        out_specs=pl.BlockSpec((None, seq, width), lambda b: (b, 0, 0)),
        out_shape=jax.ShapeDtypeStruct((n_seq, seq, width), BF16),
        compiler_params=_params(("arbitrary",), 5 * _nbytes((seq, width), F32)),
    )(u, w_pool, pool_scale.reshape(pool_scale.shape[0], 1, width))


def _split3(x):
    hi = x.astype(BF16)
    r = x - hi.astype(F32)
    mid = r.astype(BF16)
    lo = (r - mid.astype(F32)).astype(BF16)
    return hi, mid, lo


def _row_select(stack_ref, n, rows):
    row = lax.broadcasted_iota(jnp.int32, stack_ref.shape[1:], 0)
    out = jnp.zeros(stack_ref.shape[1:], F32)
    for h in range(n):
        out = jnp.where(row == h, stack_ref[h], out)
    return out


def _decode_kernel(pt_ref, qf_ref, kfn_ref, vfn_ref, lfn_ref, qd_ref, kdn_ref, vdn_ref, lq_ref, g_ref, *rest,
                   n_slots, n_fox, n_diff, head_dim, scale, lam_init):
    del pt_ref
    pages = [rest[5 * k:5 * k + 5] for k in range(n_slots)]
    (of_ref, od_ref, qbf_ref, qbd_ref, mf_ref, lf_ref, accf_ref, cf_ref, md_ref, ld_ref,
     accd_ref) = rest[5 * n_slots:]
    p = pl.program_id(1)
    rows = qf_ref.shape[0]
    row_id = lax.broadcasted_iota(jnp.int32, (rows, head_dim), 0)

    @pl.when(p == 0)
    def _init():
        qf = qf_ref[...]
        qd = qd_ref[...]
        for h in range(n_fox):
            qbf_ref[h] = jnp.where(row_id == h, qf, 0.0).astype(BF16)
            accf_ref[h] = jnp.broadcast_to(vfn_ref[h:h + 1, :], (rows, head_dim))
        for r in range(2 * n_diff):
            qbd_ref[r] = jnp.where(row_id == r, qd, 0.0).astype(BF16)
        for h in range(n_diff):
            accd_ref[h] = jnp.broadcast_to(vdn_ref[h:h + 1, :], (rows, 2 * head_dim))
        mf_ref[...] = jnp.sum(qf.astype(BF16).astype(F32) * kfn_ref[...].astype(BF16).astype(F32),
                              axis=1, keepdims=True) * scale
        md_ref[...] = jnp.sum(qd.astype(BF16).astype(F32) * kdn_ref[...].astype(BF16).astype(F32),
                              axis=1, keepdims=True) * scale
        lf_ref[...] = jnp.ones_like(lf_ref)
        ld_ref[...] = jnp.ones_like(ld_ref)
        cf_ref[...] = lfn_ref[...]

    page = pages[0][2].shape[1]
    later = (lax.broadcasted_iota(jnp.int32, (page, page), 0)
             > lax.broadcasted_iota(jnp.int32, (page, page), 1)).astype(BF16)

    def softmax_step(s, m_ref, l_ref):
        m_new = jnp.maximum(m_ref[...], jnp.max(s, axis=1, keepdims=True))
        alpha = jnp.exp(m_ref[...] - m_new)
        prob = jnp.exp(s - m_new)
        l_ref[...] = alpha * l_ref[...] + jnp.sum(prob, axis=1, keepdims=True)
        m_ref[...] = m_new
        return alpha, prob.astype(BF16)

    def scores(q_stack_ref, k_ref, n):
        s = _nt_dot(q_stack_ref[0], k_ref[0].astype(BF16))
        for r in range(1, n):
            s = s + _nt_dot(q_stack_ref[r], k_ref[r].astype(BF16))
        return s

    def weighted_values(prob, v_refs, h):
        return sum(jnp.dot(prob[:, k * page:(k + 1) * page], v_ref[h].astype(BF16), preferred_element_type=F32)
                   for k, v_ref in enumerate(v_refs))

    decay = cf_ref[...]
    s_fox, s_diff = [], []
    for ck_ref, _, clf_ref, cdk_ref, _ in pages:
        logf_t = clf_ref[...]
        suffix = sum(jnp.dot(t, later, preferred_element_type=F32) for t in _split3(logf_t))
        s_fox.append(scores(qbf_ref, ck_ref, n_fox) * scale + (decay + suffix))
        s_diff.append(scores(qbd_ref, cdk_ref, 2 * n_diff) * scale)
        decay = decay + jnp.sum(logf_t, axis=1, keepdims=True)
    cf_ref[...] = decay

    alpha, prob = softmax_step(jnp.concatenate(s_fox, axis=1), mf_ref, lf_ref)
    for h in range(n_fox):
        accf_ref[h] = alpha * accf_ref[h] + weighted_values(prob, [pg[1] for pg in pages], h)
    alpha, prob = softmax_step(jnp.concatenate(s_diff, axis=1), md_ref, ld_ref)
    for h in range(n_diff):
        accd_ref[h] = alpha * accd_ref[h] + weighted_values(prob, [pg[4] for pg in pages], h)

    @pl.when(p == pl.num_programs(1) - 1)
    def _finish():
        of_ref[...] = _row_select(accf_ref, n_fox, rows) / lf_ref[...]
        lam = _diff_lambda(lq_ref[...], lam_init)
        od_ref[...] = jnp.zeros_like(od_ref)
        for h in range(n_diff):
            a0 = accd_ref[h][2 * h:2 * h + 1, :] / ld_ref[2 * h:2 * h + 1, :]
            a1 = accd_ref[h][2 * h + 1:2 * h + 2, :] / ld_ref[2 * h + 1:2 * h + 2, :]
            od_ref[h:h + 1, :] = _sub_norm(a0 - lam * a1, g_ref[...], lam_init)


def _decode_attention(page_table, qf, kfn, vfn, lfn, qd, kdn, vdn, lambda_qk, subln_g,
                      cache_fox_k, cache_fox_v, logf_t, cache_diff_k, cache_diff_v, layer,
                      *, n_fox, n_diff, head_dim, lam_init):
    n_seq, n_pages = page_table.shape
    rows = qf.shape[1]
    page = cache_fox_k.shape[3]
    width = 2 * head_dim
    vrows = vdn.shape[1]
    n_slots = _pick(n_pages, (4, 2, 1))

    def per_seq(shape):
        return pl.BlockSpec((None,) + shape, lambda b, p, pt: (b, 0, 0))

    def paged(shape, slot):
        return pl.BlockSpec(
            (None, None) + shape,
            lambda b, p, pt: (layer, pt[b, n_pages - 1 - (p * n_slots + slot)]) + (0,) * len(shape))

    in_specs = [per_seq((rows, head_dim)), per_seq((rows, head_dim)), per_seq((rows, head_dim)),
                per_seq((rows, 1)),
                per_seq((rows, head_dim)), per_seq((rows, head_dim)), per_seq((vrows, width)),
                pl.BlockSpec((None, 4, head_dim), lambda b, p, pt: (layer, 0, 0)),
                pl.BlockSpec((None, 1, width), lambda b, p, pt: (layer, 0, 0))]
    caches = []
    for slot in range(n_slots):
        in_specs += [paged((n_fox, page, head_dim), slot), paged((n_fox, page, head_dim), slot),
                     paged((rows, page), slot),
                     paged((2 * n_diff, page, head_dim), slot), paged((n_diff, page, width), slot)]
        caches += [cache_fox_k, cache_fox_v, logf_t, cache_diff_k, cache_diff_v]
    scratch = [pltpu.VMEM((n_fox, rows, head_dim), BF16), pltpu.VMEM((2 * n_diff, rows, head_dim), BF16),
               pltpu.VMEM((rows, 1), F32), pltpu.VMEM((rows, 1), F32),
               pltpu.VMEM((n_fox, rows, head_dim), F32), pltpu.VMEM((rows, 1), F32),
               pltpu.VMEM((rows, 1), F32), pltpu.VMEM((rows, 1), F32),
               pltpu.VMEM((n_diff, rows, width), F32)]
    vmem = 2 * n_slots * (4 * _nbytes((n_fox, page, head_dim), F32) + _nbytes((rows, page), F32)) + (4 << 20)
    grid_spec = pltpu.PrefetchScalarGridSpec(
        num_scalar_prefetch=1, grid=(n_seq, n_pages // n_slots), in_specs=in_specs,
        out_specs=[pl.BlockSpec((None, rows, head_dim), lambda b, p, pt: (b, 0, 0)),
                   pl.BlockSpec((None, vrows, width), lambda b, p, pt: (b, 0, 0))],
        scratch_shapes=scratch)
    return pl.pallas_call(
        functools.partial(_decode_kernel, n_slots=n_slots, n_fox=n_fox, n_diff=n_diff, head_dim=head_dim,
                          scale=head_dim ** -0.5, lam_init=lam_init),
        grid_spec=grid_spec, name='decode_attn',
        out_shape=[jax.ShapeDtypeStruct((n_seq, rows, head_dim), F32),
                   jax.ShapeDtypeStruct((n_seq, vrows, width), F32)],
        compiler_params=_params(("arbitrary", "arbitrary"), vmem),
    )(page_table, qf, kfn, vfn, lfn, qd, kdn, vdn, lambda_qk,
      subln_g.reshape(subln_g.shape[0], 1, width), *caches)


def _rope_tables(pos, head_dim):
    half = head_dim // 2
    inv_freq = ROPE_THETA ** (-jnp.arange(half, dtype=F32) / half)
    ang = pos.astype(F32)[:, None] * inv_freq[None, :]
    cos, sin = jnp.cos(ang), jnp.sin(ang)
    return jnp.concatenate([cos, cos], axis=-1), jnp.concatenate([-sin, sin], axis=-1)


class _Tiling:
    def __init__(self, n_rows, n_groups):
        self.m = n_rows
        self.tm = _pick(n_rows, (1024, 512, 256, 128))
        self.tm_wide_k = _pick(n_rows, (256, 128))
        self.tm_norm = _pick(n_rows, (256, 128))
        self.n_groups = n_groups

    def tiles_per_group(self, tm):
        return max(1, self.m // self.n_groups // tm)


def _layer(st, x, x_t, mods, mods_t, w, layer, rope, rope_t, dims, attention, attention_t, kv_bufs):
    d = x.shape[1]
    tm, tpg = st.tm, st.tiles_per_group(st.tm)
    tmn, tpgn = st.tm_norm, st.tiles_per_group(st.tm_norm)
    rows_t = x_t.shape[0]
    d_ff = w['w_ffn_out'].shape[2]
    n_fox, n_diff, head_dim, pool_w = dims['n_fox'], dims['n_diff'], dims['head_dim'], dims['pool_w']
    fox_w = n_fox * head_dim

    def pre_norm(x, x_t, idx):
        g = w['norm_g'][layer, idx]
        h = _norm(x, g, mods[3 * idx + 1], mods[3 * idx], tm=tmn, tiles_per_group=tpgn, out_dtype=BF16)
        h_t = _norm(x_t, g, mods_t[3 * idx + 1], mods_t[3 * idx], tm=rows_t, tiles_per_group=1, out_dtype=BF16)
        return h, h_t

    def swiglu_ffn(x, x_t, which, norm_idx):
        h, h_t = pre_norm(x, x_t, norm_idx)
        tn = _pick(d_ff, (256, 128))
        lead = (layer, which)
        (hid,), (hid_t,) = _matmul(
            [h], [(0, w['w_ffn_in'], lead, 0), (0, w['w_ffn_in'], lead, d_ff // tn)], [],
            _ep_swiglu, [BF16], tm=tm, tn=tn, n_tiles=d_ff // tn, name='ffn_in', tail=([h_t], []))
        tm2 = st.tm_wide_k
        tn2 = _pick(d, (512, 256, 128))
        gate, gate_t = mods[3 * norm_idx + 2], mods_t[3 * norm_idx + 2]
        (x_new,), (x_new_t,) = _matmul(
            [hid], [(0, w['w_ffn_out'], lead, 0)],
            [_tile_extra(x, tm2, tn2), _mod_extra(gate, tn2, st.tiles_per_group(tm2))],
            functools.partial(_ep_residual, 0.5), [F32], tm=tm2, tn=tn2, n_tiles=d // tn2,
            name='ffn_out', single_buffer_weights=True,
            tail=([hid_t], [_tail_tile(x_t, tn2), _tail_mod(gate_t, tn2)]))
        return x_new, x_new_t

    x, x_t = swiglu_ffn(x, x_t, 0, 0)

    h, h_t = pre_norm(x, x_t, 1)
    tn = _pick(fox_w, (512, 256, 128))
    nt = fox_w // tn
    w_in = w['w_in']

    def proj(weight, lead, col0, n_tiles, out_dtype, rope_it=False, buf=None):
        if rope_it:
            tiles = rope[0].shape[0] // tm
            extras = [(t, (tm, head_dim), lambda j, i: (i % tiles, 0)) for t in rope]
            extras_t = [(t, (rows_t, head_dim), lambda j, i: (0, 0)) for t in rope_t]
            ep = functools.partial(_ep_rope, head_dim)
        else:
            extras, extras_t, ep = [], [], _ep_plain
        (out,), (out_t,) = _matmul(
            [h], [(0, weight, lead, col0)], extras, ep, [out_dtype], tm=tm, tn=tn, n_tiles=n_tiles,
            name='proj_rope' if rope_it else 'proj', head_major=None if buf is None else (buf, layer, tpg),
            tail=([h_t], extras_t))
        return out, out_t

    fk_buf, fv_buf, dk_buf, dv_buf = kv_bufs
    fq, fq_t = proj(w_in, (layer,), 0, nt, BF16)
    fk, fk_t = proj(w_in, (layer,), nt, nt, F32, buf=fk_buf)
    fv, fv_t = proj(w_in, (layer,), 2 * nt, nt, F32, buf=fv_buf)
    w_d = w['w_in_diff']
    dq, dq_t = proj(w_d, (layer,), 0, nt, BF16, rope_it=True)
    dk, dk_t = proj(w_d, (layer,), nt, nt, F32, rope_it=True, buf=dk_buf)
    dv, dv_t = proj(w_d, (layer,), 2 * nt, nt, F32, buf=dv_buf)
    u, u_t = proj(w_d, (layer,), 3 * nt, pool_w // tn, F32)
    (z,), (z_t,) = _matmul([h], [(0, w['w_in_forget'], (layer,), 0)], [], _ep_plain, [F32],
                           tm=tm, tn=LANES_V7X, n_tiles=1, name='proj_forget', tail=([h_t], []))

    y_fox, y_diff, y_pool, logf, u_state = attention(fq, fk, fv, z, dq, dk, dv, u)
    y_fox_t, y_diff_t, y_pool_t, logf_t, u_state_t = attention_t(fq_t, fk_t, fv_t, z_t, dq_t, dk_t, dv_t, u_t)

    tng = _pick(d, (512, 256, 128))
    ntg = d // tng
    bias = (w['b_gate'].reshape(-1, 1, 3 * d), (None, 1, tng), lambda j, i: (layer, 0, j))
    (gates,), (gates_t,) = _matmul([h], [(0, w['w_gate'], (layer,), 0)], [bias],
                                   _ep_sigmoid_bias, [BF16], tm=tm, tn=tng, n_tiles=3 * ntg, name='gates',
                                   tail=([h_t], [bias]))
    (merged,), (merged_t,) = _matmul(
        [y_fox, y_diff, y_pool],
        [(0, w['w_br_fox'], (layer,), 0), (1, w['w_br_diff'], (layer,), 0), (2, w['w_br_pool'], (layer,), 0)],
        [_tile_extra(gates, tm, tng, b * ntg) for b in range(3)],
        _ep_merge, [BF16], tm=tm, tn=tng, n_tiles=ntg, name='merge',
        tail=([y_fox_t, y_diff_t, y_pool_t], [_tail_tile(gates_t, tng, b * ntg) for b in range(3)]))
    (x,), (x_t,) = _matmul([merged], [(0, w['w_o'], (layer,), 0)],
                           [_tile_extra(x, tm, tng), _mod_extra(mods[5], tng, tpg)],
                           functools.partial(_ep_residual, 1.0), [F32], tm=tm, tn=tng, n_tiles=ntg, name='w_o',
                           tail=([merged_t], [_tail_tile(x_t, tng), _tail_mod(mods_t[5], tng)]))

    x, x_t = swiglu_ffn(x, x_t, 1, 2)
    return x, x_t, (fk, fv, logf, dk, dv, u_state), (fk_t, fv_t, logf_t, dk_t, dv_t, u_state_t)


def kernel(x_prompt, x_sample, c_prompt, c_sample, page_table, cache_fox_k, cache_fox_v, cache_fox_logf, cache_diff_k, cache_diff_v, state_pool, norm_g, w_ada, b_ada, w_ffn_in, w_ffn_out, w_in, b_forget, lambda_qk, subln_g, w_pool, pool_scale, w_br_fox, w_br_diff, w_br_pool, w_gate, b_gate, w_o, final_norm_g):
    n_seq, seq, d = x_prompt.shape
    n_dec, dec_seq, _ = x_sample.shape
    assert dec_seq == 1 and n_dec <= SAMPLE_ROWS
    depth = norm_g.shape[0]
    n_fox, head_dim = cache_fox_k.shape[3], cache_fox_k.shape[4]
    n_diff = cache_diff_v.shape[3]
    fox_w = n_fox * head_dim
    pool_w = w_pool.shape[1] * w_pool.shape[2]
    pool_state = state_pool.shape[2]
    n_pages, page = page_table.shape[1], cache_fox_k.shape[2]
    past_len = n_pages * page
    dims = dict(n_fox=n_fox, n_diff=n_diff, head_dim=head_dim, pool_w=pool_w)
    rows = SAMPLE_ROWS
    assert n_fox <= rows and 2 * n_diff <= rows and n_fox <= LANES_V7X

    off_ff = 3 * fox_w
    off_d = off_ff + n_fox
    weights = dict(norm_g=norm_g, w_ffn_in=w_ffn_in, w_ffn_out=w_ffn_out, w_in=w_in[:, :, :off_ff], w_gate=w_gate,
                   b_gate=b_gate, w_br_fox=w_br_fox, w_br_diff=w_br_diff, w_br_pool=w_br_pool, w_o=w_o,
                   w_in_diff=w_in[:, :, off_d:],
                   w_in_forget=jnp.pad(w_in[:, :, off_ff:off_d], ((0, 0), (0, 0), (0, LANES_V7X - n_fox))))
    b_forget_pad = jnp.pad(b_forget, ((0, 0), (0, LANES_V7X - n_fox)))

    n_c = n_seq + n_dec
    c_rows = -(-n_c // BF16_ROWS_V7X) * BF16_ROWS_V7X
    c_all = jnp.pad(jnp.concatenate([c_prompt, c_sample], axis=0), ((0, c_rows - n_c), (0, 0)))
    tn_ada = _pick(N_MOD * d, (1024, 512, 256, 128))
    mods_p, mods_s = [], []
    for l in range(depth):
        (mod,) = _matmul([c_all], [(0, w_ada, (l,), 0)],
                         [(b_ada.reshape(depth, 1, N_MOD * d), (None, 1, tn_ada), lambda j, i, l=l: (l, 0, j))],
                         _ep_bias, [F32], tm=c_rows, tn=tn_ada, n_tiles=N_MOD * d // tn_ada, name='ada',
                         prologue=_silu_bf16)
        mod = mod.reshape(c_rows, N_MOD, d)
        mods_p.append([mod[:n_seq, k][:, None, :] for k in range(N_MOD)])
        mods_s.append([jnp.pad(mod[n_seq:n_c, k], ((0, rows - n_dec), (0, 0)))[None] for k in range(N_MOD)])

    st_p = _Tiling(n_seq * seq, n_seq)
    rope_p = _rope_tables(jnp.arange(seq, dtype=jnp.int32), head_dim)
    rope_s = tuple(jnp.broadcast_to(t, (rows, head_dim))
                   for t in _rope_tables(jnp.full((1,), past_len, jnp.int32), head_dim))
    logf_t = jnp.pad(jnp.swapaxes(cache_fox_logf, 2, 3), ((0, 0), (0, 0), (0, rows - n_fox), (0, 0)))
    head_major = (0, 1, 3, 2, 4)
    cache_views = [jnp.transpose(c, head_major) for c in (cache_fox_k, cache_fox_v, cache_diff_k, cache_diff_v)]
    kv_bufs = [jnp.zeros((depth, n_seq, n_fox, seq, head_dim), F32),
               jnp.zeros((depth, n_seq, n_fox, seq, head_dim), F32),
               jnp.zeros((depth, n_seq, 2 * n_diff, seq, head_dim), F32),
               jnp.zeros((depth, n_seq, n_diff, seq, 2 * head_dim), F32)]

    def attention_prompt(l, fq, fk, fv, z, dq, dk, dv, u):
        logf, cum_cols, cum_rows = _logf(z, b_forget_pad[l:l + 1], n_seq=n_seq, seq=seq, n_heads=n_fox,
                                         cumulative=True)
        y_fox = _fox_attention(fq, fk, fv, l, cum_cols, cum_rows, n_seq=n_seq, seq=seq, n_heads=n_fox,
                               head_dim=head_dim)
        lam_init = 0.8 - 0.6 * math.exp(-0.3 * l)
        y_diff = _diff_attention(dq, dk, dv, lambda_qk, subln_g, l, n_seq=n_seq, seq=seq, n_heads=n_diff,
                                 head_dim=head_dim, lam_init=lam_init)
        u3 = u.reshape(n_seq, seq, pool_w)
        y_pool = _pool(u3, w_pool, pool_scale, l).reshape(n_seq * seq, pool_w)
        return y_fox, y_diff, y_pool, logf, u3[:, seq - pool_state:]

    def attention_sample(l, fq, fk, fv, z, dq, dk, dv, u):
        logf, cum_cols, _ = _logf(z, b_forget_pad[l:l + 1], n_seq=1, seq=rows, n_heads=n_fox, cumulative=False)

        def heads(a, n, width):
            a = a[:n_dec].astype(F32).reshape(n_dec, n, width)
            return jnp.pad(a, ((0, 0), (0, -(-n // SUBLANES_V7X) * SUBLANES_V7X - n), (0, 0)))

        def heads16(a, n):
            a = heads(a, n, head_dim)
            return jnp.pad(a, ((0, 0), (0, rows - a.shape[1]), (0, 0)))

        lfn = jnp.pad(cum_cols[:n_dec, :n_fox], ((0, 0), (0, rows - n_fox)))[:, :, None]
        lam_init = 0.8 - 0.6 * math.exp(-0.3 * l)
        o_fox, o_diff = _decode_attention(
            page_table, heads16(fq, n_fox), heads16(fk, n_fox), heads16(fv, n_fox), lfn,
            heads16(dq, 2 * n_diff), heads16(dk, 2 * n_diff), heads(dv, n_diff, 2 * head_dim),
            lambda_qk, subln_g, cache_views[0], cache_views[1], logf_t, cache_views[2], cache_views[3], l,
            n_fox=n_fox, n_diff=n_diff, head_dim=head_dim, lam_init=lam_init)

        def unheads(o, n):
            o = o[:, :n].reshape(n_dec, -1)
            return jnp.pad(o, ((0, rows - n_dec), (0, 0))).astype(BF16)

        u_ext = jnp.concatenate([state_pool[l], u[:n_dec, None, :]], axis=1)
        y_pool = _pool(u_ext, w_pool, pool_scale, l)[:, -1]
        y_pool = jnp.pad(y_pool, ((0, rows - n_dec), (0, 0)))
        return unheads(o_fox, n_fox), unheads(o_diff, n_diff), y_pool, logf, u_ext[:, 1:]

    xp = x_prompt.reshape(n_seq * seq, d)
    xs = jnp.pad(x_sample.reshape(n_dec, d), ((0, rows - n_dec), (0, 0)))
    states_p, states_s = [], []
    for l in range(depth):
        xp, xs, sp, ss = _layer(st_p, xp, xs, mods_p[l], mods_s[l], weights, l, rope_p, rope_s, dims,
                                functools.partial(attention_prompt, l), functools.partial(attention_sample, l),
                                kv_bufs)
        kv_bufs = [sp[0], sp[1], sp[3], sp[4]]
        states_p.append(sp)
        states_s.append(ss)

    y_prompt = _norm(xp, final_norm_g, None, None, tm=st_p.tm_norm, tiles_per_group=1, out_dtype=F32)
    y_sample = _norm(xs, final_norm_g, None, None, tm=rows, tiles_per_group=1, out_dtype=F32)

    def stack_p(i, shape):
        return jnp.stack([s[i].reshape(shape) for s in states_p], axis=0)

    def stack_s(i, shape):
        return jnp.stack([s[i][:n_dec].reshape(shape) for s in states_s], axis=0)

    return (y_prompt.reshape(n_seq, seq, d), y_sample[:n_dec].reshape(n_dec, 1, d),
            jnp.transpose(kv_bufs[0], head_major), jnp.transpose(kv_bufs[1], head_major),
            stack_p(2, (n_seq, seq, n_fox)),
            jnp.transpose(kv_bufs[2], head_major), jnp.transpose(kv_bufs[3], head_major),
            jnp.stack([s[5] for s in states_p], axis=0),
            stack_s(0, (n_dec, 1, n_fox, head_dim)), stack_s(1, (n_dec, 1, n_fox, head_dim)),
            stack_s(2, (n_dec, 1, n_fox)),
            stack_s(3, (n_dec, 1, 2 * n_diff, head_dim)), stack_s(4, (n_dec, 1, n_diff, 2 * head_dim)),
            jnp.stack([s[5] for s in states_s], axis=0))
```
